```python
import math
import jax, jax.numpy as jnp
from jax import lax
import numpy as np


D_MODEL = 1024
BATCH = 4
SEQ = 8192
DEPTH = 4

MIX_WIDTH = D_MODEL // 2
N_BRANCH = 4
EPS = 1e-6

RWKV_HEAD_DIM = 64
RWKV_HEADS = MIX_WIDTH // RWKV_HEAD_DIM
RWKV_DECAY_LORA = 64
RWKV_A_LORA = 64
RWKV_G_LORA = 128
RWKV_DECAY_SCALE = 0.6065306597
RWKV_GN_EPS = 64e-5
GLA_HEADS = 4
GLA_DK = MIX_WIDTH // 2
GLA_DV = MIX_WIDTH
GLA_GATE_LORA = 16
GLA_TAU = 16.0
GLA_CHUNK = 16
MLSTM_HEADS = 4
MLSTM_DQK = MIX_WIDTH // 2
MLSTM_DV = MIX_WIDTH
MLSTM_CONV = 4
MLSTM_CHUNK = 64
SSD_HEADS = 8
SSD_HEAD_DIM = MIX_WIDTH // SSD_HEADS
SSD_GROUPS = 2
SSD_HPG = SSD_HEADS // SSD_GROUPS
SSD_STATE = 128
SSD_CONV = 4
SSD_CHUNK = 128
SSD_CONV_DIM = MIX_WIDTH + 2 * SSD_GROUPS * SSD_STATE
D_FF = 2816
FFN_CONV = 3

RWKV_IN = 3 * MIX_WIDTH + RWKV_DECAY_LORA + RWKV_A_LORA + RWKV_G_LORA
GLA_IN = 2 * GLA_DK + 2 * GLA_DV + GLA_GATE_LORA
MLSTM_IN = 2 * MLSTM_DQK + 2 * MLSTM_DV + 2 * MLSTM_HEADS
SSD_IN = MIX_WIDTH + SSD_CONV_DIM + SSD_HEADS
GATE_IN = N_BRANCH * D_MODEL
MIXER_IN_SIZES = (RWKV_IN, GLA_IN, MLSTM_IN, SSD_IN, GATE_IN)
D_IN = RWKV_IN + GLA_IN + MLSTM_IN + SSD_IN + GATE_IN

kernel_name = 'hybrid_parallel_mixer_trunk'


def _split(a, sizes):
    idx = np.cumsum(np.array(sizes))[:-1].tolist()
    return jnp.split(a, idx, axis=-1)


def rms_norm(x, g):
    xf = x.astype(jnp.float32)
    return xf * lax.rsqrt(jnp.mean(xf * xf, axis=-1, keepdims=True) + EPS) * g


def head_rms_norm(y, g):
    y = y.astype(jnp.float32)
    return y * lax.rsqrt(jnp.mean(y * y, axis=-1, keepdims=True) + EPS) * g


def head_group_norm(y, g, b, eps):
    y = y.astype(jnp.float32)
    yc = y - jnp.mean(y, axis=-1, keepdims=True)
    return yc * lax.rsqrt(jnp.mean(yc * yc, axis=-1, keepdims=True) + eps) * g + b


def causal_dwconv(x, w, b):
    k, c = w.shape
    y = lax.conv_general_dilated(x, w[:, None, :].astype(x.dtype), window_strides=(1,),
                                 padding=((k - 1, 0),), dimension_numbers=('NWC', 'WIO', 'NWC'),
                                 feature_group_count=c)
    return y + b


def token_shift(f, mu):
    prev = jnp.pad(f, ((0, 0), (1, 0), (0, 0)))[:, :-1]
    return f + mu * (prev - f)


def to_chunks(a, size):
    bsz, t = a.shape[:2]
    return jnp.moveaxis(a.reshape((bsz, t // size, size) + a.shape[2:]), 1, 0)


def from_chunks(a):
    nc, bsz, size = a.shape[:3]
    return jnp.moveaxis(a, 0, 1).reshape((bsz, nc * size) + a.shape[3:])


def causal_mask(size):
    return jnp.tril(jnp.ones((size, size), dtype=bool))


def rwkv7_mix(fa, mu, w0, w_up, a0, a_up, g_up, k_k, k_a, r_k, gn_g, gn_b):
    bsz, t, _ = fa.shape
    fa = token_shift(fa.astype(jnp.float32), mu)
    r, k, v, w_lo, a_lo, g_lo = _split(fa, (MIX_WIDTH, MIX_WIDTH, MIX_WIDTH, RWKV_DECAY_LORA, RWKV_A_LORA, RWKV_G_LORA))
    heads = lambda z: z.reshape(bsz, t, RWKV_HEADS, RWKV_HEAD_DIM)
    log_decay = -RWKV_DECAY_SCALE * jax.nn.sigmoid(w0 + jnp.tanh(w_lo) @ w_up)
    a = jax.nn.sigmoid(a0 + a_lo @ a_up)
    g = jax.nn.sigmoid(g_lo) @ g_up
    kk = heads(k * k_k)
    kk = kk / jnp.maximum(jnp.sqrt(jnp.sum(kk * kk, axis=-1, keepdims=True)), 1e-12)
    k = k * (1.0 + (a - 1.0) * k_a)
    r, k, v, a, decay = heads(r), heads(k), heads(v), heads(a), jnp.exp(heads(log_decay))

    def step(s, inp):
        r_t, w_t, k_t, v_t, kk_t, a_t = inp
        sa = jnp.einsum('bhvk,bhk->bhv', s, -kk_t)
        s = (s * w_t[:, :, None, :] + sa[..., None] * (kk_t * a_t)[:, :, None, :]
             + v_t[..., None] * k_t[:, :, None, :])
        return s, jnp.einsum('bhvk,bhk->bhv', s, r_t)

    time_major = lambda z: jnp.moveaxis(z, 1, 0)
    s0 = jnp.zeros((bsz, RWKV_HEADS, RWKV_HEAD_DIM, RWKV_HEAD_DIM), jnp.float32)
    _, y = lax.scan(step, s0, tuple(time_major(z) for z in (r, decay, k, v, kk, a)))
    y = jnp.moveaxis(y, 0, 1)
    y = head_group_norm(y, gn_g.reshape(RWKV_HEADS, RWKV_HEAD_DIM), gn_b.reshape(RWKV_HEADS, RWKV_HEAD_DIM), RWKV_GN_EPS)
    y = y + jnp.sum(r * k * r_k, axis=-1, keepdims=True) * v
    return y.reshape(bsz, t, MIX_WIDTH) * g


def gla_mix(fb, f_up, f_bias, norm_g):
    bsz, t, _ = fb.shape
    dk, dv = GLA_DK // GLA_HEADS, GLA_DV // GLA_HEADS
    q, k, v, f_lo, og = _split(fb.astype(jnp.float32), (GLA_DK, GLA_DK, GLA_DV, GLA_GATE_LORA, GLA_DV))
    hk = lambda z: z.reshape(bsz, t, GLA_HEADS, dk)
    log_alpha = hk(jax.nn.log_sigmoid(f_lo @ f_up + f_bias) / GLA_TAU)
    q = hk(q) * dk ** -0.5
    k = hk(k)
    v = v.reshape(bsz, t, GLA_HEADS, dv)
    mask = causal_mask(GLA_CHUNK)

    def step(s, inp):
        q_c, k_c, v_c, la_c = inp
        b = jnp.cumsum(la_c, axis=1)
        b_last = b[:, -1]
        qd = q_c * jnp.exp(b)
        kd = k_c * jnp.exp(-b)
        att = jnp.where(mask, jnp.einsum('blhd,bshd->bhls', qd, kd), 0.0)
        o = jnp.einsum('bhls,bshe->blhe', att, v_c) + jnp.einsum('blhd,bhde->blhe', qd, s)
        s = jnp.exp(b_last)[..., None] * s + jnp.einsum('bshd,bshe->bhde', k_c * jnp.exp(b_last[:, None] - b), v_c)
        return s, o

    s0 = jnp.zeros((bsz, GLA_HEADS, dk, dv), jnp.float32)
    _, o = lax.scan(step, s0, tuple(to_chunks(z, GLA_CHUNK) for z in (q, k, v, log_alpha)))
    o = head_rms_norm(from_chunks(o), norm_g)
    return o.reshape(bsz, t, GLA_DV) * jax.nn.silu(og)


def mlstm_mix(fc, conv_w, conv_b, i_bias, f_bias, norm_g):
    bsz, t, _ = fc.shape
    dqk, dv = MLSTM_DQK // MLSTM_HEADS, MLSTM_DV // MLSTM_HEADS
    qk, v, i_pre, f_pre, og = _split(fc.astype(jnp.float32), (2 * MLSTM_DQK, MLSTM_DV, MLSTM_HEADS, MLSTM_HEADS, MLSTM_DV))
    q, k = _split(jax.nn.silu(causal_dwconv(qk, conv_w, conv_b)), (MLSTM_DQK, MLSTM_DQK))
    q = q.reshape(bsz, t, MLSTM_HEADS, dqk)
    k = k.reshape(bsz, t, MLSTM_HEADS, dqk) * dqk ** -0.5
    v = v.reshape(bsz, t, MLSTM_HEADS, dv)
    log_i = i_pre + i_bias
    log_f = jax.nn.log_sigmoid(f_pre + f_bias)
    mask = causal_mask(MLSTM_CHUNK)

    def step(carry, inp):
        c, n, m = carry
        q_c, k_c, v_c, li_c, lf_c = inp
        b = jnp.swapaxes(jnp.cumsum(lf_c, axis=1), 1, 2)
        li_c = jnp.swapaxes(li_c, 1, 2)
        log_d = jnp.where(mask, b[..., :, None] - b[..., None, :] + li_c[..., None, :], -jnp.inf)
        m_t = jnp.maximum(b + m[..., None], jnp.max(log_d, axis=-1))
        d = jnp.exp(log_d - m_t[..., None])
        inter = jnp.exp(b + m[..., None] - m_t)
        sc = jnp.einsum('blhd,bshd->bhls', q_c, k_c) * d
        num = (jnp.einsum('bhls,bshe->blhe', sc, v_c)
               + jnp.einsum('blhd,bhde->blhe', q_c, c) * jnp.swapaxes(inter, 1, 2)[..., None])
        den = jnp.sum(sc, axis=-1) + inter * jnp.einsum('blhd,bhd->bhl', q_c, n)
        h = num / jnp.swapaxes(jnp.maximum(jnp.abs(den), jnp.exp(-m_t)), 1, 2)[..., None]
        w_last, f_last = d[..., -1, :], inter[..., -1]
        c = f_last[..., None, None] * c + jnp.einsum('bhs,bshd,bshe->bhde', w_last, k_c, v_c)
        n = f_last[..., None] * n + jnp.einsum('bhs,bshd->bhd', w_last, k_c)
        return (c, n, m_t[..., -1]), h

    carry0 = (jnp.zeros((bsz, MLSTM_HEADS, dqk, dv), jnp.float32),
              jnp.zeros((bsz, MLSTM_HEADS, dqk), jnp.float32),
              jnp.zeros((bsz, MLSTM_HEADS), jnp.float32))
    _, h = lax.scan(step, carry0, tuple(to_chunks(z, MLSTM_CHUNK) for z in (q, k, v, log_i, log_f)))
    h = head_rms_norm(from_chunks(h), norm_g.reshape(MLSTM_HEADS, dv))
    return h.reshape(bsz, t, MLSTM_DV) * jax.nn.sigmoid(og)


def ssd_mix(fd, conv_w, conv_b, dt_bias, a_log, d_skip, norm_g):
    bsz, t, _ = fd.shape
    z, xbc, dt = _split(fd.astype(jnp.float32), (MIX_WIDTH, SSD_CONV_DIM, SSD_HEADS))
    x, bm, cm = _split(jax.nn.silu(causal_dwconv(xbc, conv_w, conv_b)),
                       (MIX_WIDTH, SSD_GROUPS * SSD_STATE, SSD_GROUPS * SSD_STATE))
    x = x.reshape(bsz, t, SSD_GROUPS, SSD_HPG, SSD_HEAD_DIM)
    bm = bm.reshape(bsz, t, SSD_GROUPS, SSD_STATE)
    cm = cm.reshape(bsz, t, SSD_GROUPS, SSD_STATE)
    dt = jax.nn.softplus(dt + dt_bias).reshape(bsz, t, SSD_GROUPS, SSD_HPG)
    a = -jnp.exp(a_log).reshape(SSD_GROUPS, SSD_HPG)
    mask = causal_mask(SSD_CHUNK)

    def step(s, inp):
        x_c, dt_c, b_c, c_c = inp
        acum = jnp.cumsum(dt_c * a, axis=1)
        ah = jnp.moveaxis(acum, 1, -1)
        seg = jnp.exp(jnp.where(mask, ah[..., :, None] - ah[..., None, :], -jnp.inf))
        xdt = x_c * dt_c[..., None]
        cb = jnp.einsum('blgn,bsgn->bgls', c_c, b_c)
        y = (jnp.einsum('bgls,bgrls,bsgrp->blgrp', cb, seg, xdt)
             + jnp.einsum('blgn,bgrpn,blgr->blgrp', c_c, s, jnp.exp(acum)))
        last = acum[:, -1]
        s = (jnp.exp(last)[..., None, None] * s
             + jnp.einsum('bsgn,bsgr,bsgrp->bgrpn', b_c, jnp.exp(last[:, None] - acum), xdt))
        return s, y

    s0 = jnp.zeros((bsz, SSD_GROUPS, SSD_HPG, SSD_HEAD_DIM, SSD_STATE), jnp.float32)
    _, y = lax.scan(step, s0, tuple(to_chunks(zz, SSD_CHUNK) for zz in (x, dt, bm, cm)))
    y = from_chunks(y) + x * d_skip.reshape(SSD_GROUPS, SSD_HPG)[..., None]
    return rms_norm(y.reshape(bsz, t, MIX_WIDTH) * jax.nn.silu(z), norm_g)


def setup_inputs(seed: int = 0) -> dict:
    key = jax.random.key(seed)
    ks = iter(jax.random.split(key, 48))

    def nrm(shape, scale):
        return scale * jax.random.normal(next(ks), shape, jnp.float32)

    def gain(shape):
        return 1.0 + nrm(shape, 0.02)

    L = DEPTH
    dt0 = jnp.exp(jax.random.uniform(next(ks), (L, SSD_HEADS), jnp.float32, math.log(1e-3), math.log(1e-1)))
    return {
        'x': nrm((BATCH, SEQ, D_MODEL), 1.0),
        'mix_norm_g': gain((L, D_MODEL)),
        'w_in': nrm((L, D_MODEL, D_IN), D_MODEL ** -0.5),
        'rwkv_mu': jax.random.uniform(next(ks), (L, RWKV_IN), jnp.float32),
        'rwkv_w0': nrm((L, MIX_WIDTH), 1.0),
        'rwkv_w_up': nrm((L, RWKV_DECAY_LORA, MIX_WIDTH), RWKV_DECAY_LORA ** -0.5),
        'rwkv_a0': nrm((L, MIX_WIDTH), 0.5),
        'rwkv_a_up': nrm((L, RWKV_A_LORA, MIX_WIDTH), 0.5 * RWKV_A_LORA ** -0.5),
        'rwkv_g_up': nrm((L, RWKV_G_LORA, MIX_WIDTH), RWKV_G_LORA ** -0.5),
        'rwkv_k_k': 0.85 + nrm((L, MIX_WIDTH), 0.02),
        'rwkv_k_a': 1.0 + nrm((L, MIX_WIDTH), 0.02),
        'rwkv_r_k': nrm((L, RWKV_HEADS, RWKV_HEAD_DIM), 0.1),
        'rwkv_gn_g': gain((L, MIX_WIDTH)),
        'rwkv_gn_b': nrm((L, MIX_WIDTH), 0.02),
        'gla_f_up': nrm((L, GLA_GATE_LORA, GLA_DK), GLA_GATE_LORA ** -0.5),
        'gla_f_bias': nrm((L, GLA_DK), 0.5),
        'gla_norm_g': gain((L, GLA_DV // GLA_HEADS)),
        'mlstm_conv_w': nrm((L, MLSTM_CONV, 2 * MLSTM_DQK), MLSTM_CONV ** -0.5),
        'mlstm_conv_b': nrm((L, 2 * MLSTM_DQK), 0.02),
        'mlstm_i_bias': nrm((L, MLSTM_HEADS), 0.1),
        'mlstm_f_bias': jnp.linspace(3.0, 6.0, MLSTM_HEADS) + nrm((L, MLSTM_HEADS), 0.1),
        'mlstm_norm_g': gain((L, MLSTM_DV)),
        'ssd_conv_w': nrm((L, SSD_CONV, SSD_CONV_DIM), SSD_CONV ** -0.5),
        'ssd_conv_b': nrm((L, SSD_CONV_DIM), 0.02),
        'ssd_dt_bias': dt0 + jnp.log(-jnp.expm1(-dt0)),
        'ssd_a_log': jnp.log(jax.random.uniform(next(ks), (L, SSD_HEADS), jnp.float32, 1.0, 16.0)),
        'ssd_d': gain((L, SSD_HEADS)),
        'ssd_norm_g': gain((L, MIX_WIDTH)),
        'branch_proj': nrm((L, N_BRANCH, MIX_WIDTH, D_MODEL), MIX_WIDTH ** -0.5),
        'w_out': nrm((L, D_MODEL, D_MODEL), D_MODEL ** -0.5),
        'ffn_norm_g': gain((L, D_MODEL)),
        'ffn_up': nrm((L, D_MODEL, 2 * D_FF), D_MODEL ** -0.5),
        'ffn_conv_w': nrm((L, FFN_CONV, 2 * D_FF), FFN_CONV ** -0.5),
        'ffn_conv_b': nrm((L, 2 * D_FF), 0.02),
        'ffn_down': nrm((L, D_FF, D_MODEL), D_FF ** -0.5),
        'final_norm_g': gain((D_MODEL,)),
    }


def reference(x, mix_norm_g, w_in, rwkv_mu, rwkv_w0, rwkv_w_up, rwkv_a0, rwkv_a_up, rwkv_g_up,
              rwkv_k_k, rwkv_k_a, rwkv_r_k, rwkv_gn_g, rwkv_gn_b, gla_f_up, gla_f_bias, gla_norm_g,
              mlstm_conv_w, mlstm_conv_b, mlstm_i_bias, mlstm_f_bias, mlstm_norm_g,
              ssd_conv_w, ssd_conv_b, ssd_dt_bias, ssd_a_log, ssd_d, ssd_norm_g,
              branch_proj, w_out, ffn_norm_g, ffn_up, ffn_conv_w, ffn_conv_b, ffn_down, final_norm_g):
    bsz, t, _ = x.shape
    h = x
    for l in range(DEPTH):
        u = rms_norm(h, mix_norm_g[l])
        fa, fb, fc, fd, fg = _split(u @ w_in[l], MIXER_IN_SIZES)
        branches = (
            rwkv7_mix(fa, rwkv_mu[l], rwkv_w0[l], rwkv_w_up[l], rwkv_a0[l], rwkv_a_up[l], rwkv_g_up[l],
                      rwkv_k_k[l], rwkv_k_a[l], rwkv_r_k[l], rwkv_gn_g[l], rwkv_gn_b[l]),
            gla_mix(fb, gla_f_up[l], gla_f_bias[l], gla_norm_g[l]),
            mlstm_mix(fc, mlstm_conv_w[l], mlstm_conv_b[l], mlstm_i_bias[l], mlstm_f_bias[l], mlstm_norm_g[l]),
            ssd_mix(fd, ssd_conv_w[l], ssd_conv_b[l], ssd_dt_bias[l], ssd_a_log[l], ssd_d[l], ssd_norm_g[l]),
        )
        gates = jax.nn.sigmoid(fg.astype(jnp.float32)).reshape(bsz, t, N_BRANCH, D_MODEL)
        merged = gates[:, :, 0] * (branches[0] @ branch_proj[l, 0])
        for i in range(1, N_BRANCH):
            merged = merged + gates[:, :, i] * (branches[i] @ branch_proj[l, i])
        h = h + merged @ w_out[l]
        u = rms_norm(h, ffn_norm_g[l])
        gate, val = _split(causal_dwconv(u @ ffn_up[l], ffn_conv_w[l], ffn_conv_b[l]), (D_FF, D_FF))
        h = h + (jax.nn.silu(gate) * val) @ ffn_down[l]
    return rms_norm(h, final_norm_g).astype(x.dtype)
```

```python
import functools

import numpy as np
import jax
import jax.numpy as jnp
from jax import lax
from jax.experimental import pallas as pl
from jax.experimental.pallas import tpu as pltpu

F32, BF16 = jnp.float32, jnp.bfloat16

D_MODEL = 1024
MIX = 512
EPS = 1e-6
N_BRANCH = 4
RWKV_DECAY_SCALE = 0.6065306597
RWKV_GN_EPS = 64e-5
RWKV_CHUNK = 64
GLA_CHUNK = 16
MLSTM_CHUNK = 64
SSD_CHUNK = 128
D_FF = 2816

RWKV_IN, GLA_IN, MLSTM_IN, SSD_IN = 1792, 1552, 1544, 1544
OFF_GLA = RWKV_IN
OFF_MLSTM = OFF_GLA + GLA_IN
OFF_SSD = OFF_MLSTM + MLSTM_IN
OFF_GATE = OFF_SSD + SSD_IN

LANES = 128
CARRY_ROWS = 8
MIXER_TILE = 256
VMEM_LIMIT = 48 * 1024 * 1024


def _mm(a, b):
    return jnp.dot(a.astype(BF16), b.astype(BF16), preferred_element_type=F32)


def _mm_nt(a, b):
    return lax.dot_general(a.astype(BF16), b.astype(BF16), (((1,), (1,)), ((), ())),
                           preferred_element_type=F32)


def _mm_tn(a, b):
    return lax.dot_general(a.astype(BF16), b.astype(BF16), (((0,), (0,)), ((), ())),
                           preferred_element_type=F32)


def _bf16_terms(x, n):
    terms, rest = [], x
    for _ in range(n):
        t = rest.astype(BF16)
        terms.append(t)
        rest = rest - t.astype(F32)
    return terms


def _sel_lhs(sel, x, n=3):
    out = None
    for t in _bf16_terms(x, n):
        p = jnp.dot(sel, t, preferred_element_type=F32)
        out = p if out is None else out + p
    return out


def _sel_rhs(x, sel, n=2):
    out = None
    for t in _bf16_terms(x, n):
        p = jnp.dot(t, sel, preferred_element_type=F32)
        out = p if out is None else out + p
    return out


def _rms(x, g):
    return x * lax.rsqrt(jnp.mean(x * x, axis=-1, keepdims=True) + EPS) * g


def _silu(x):
    return x * jax.nn.sigmoid(x)


def _softplus(x):
    return jnp.maximum(x, 0.0) + jnp.log1p(jnp.exp(-jnp.abs(x)))


def _log_sigmoid(x):
    return -_softplus(-x)


def _shift_rows(x, carry, s, rows):
    y = pltpu.roll(x, s, 0)
    for q in range(s):
        src = CARRY_ROWS - s + q
        y = jnp.where(rows == q, carry[src:src + 1, :], y)
    return y


def _causal_conv(x, carry, w, b, rows):
    k = w.shape[0]
    y = w[k - 1:k, :] * x + b
    for s in range(1, k):
        y = y + w[k - 1 - s:k - s, :] * _shift_rows(x, carry, s, rows)
    return y


def _rwkv_kernel(h_ref, ng_ref, w_ref, mu_ref, w0_ref, a0_ref, lora_ref, gup_ref, kk_ref, ka_ref,
                 rk_ref, gng_ref, gnb_ref, bd_ref, tri_ref, blk_ref, out_ref,
                 st_ref, prev_ref, ab_s, rb_s, bt_s, kt_s, v_s, bg_s, kg_s, gl_s, y_s, bon_s, g_s,
                 *, tt):
    L = RWKV_CHUNK

    @pl.when(pl.program_id(1) == 0)
    def _init():
        st_ref[...] = jnp.zeros_like(st_ref)
        prev_ref[...] = jnp.zeros_like(prev_ref)

    u = _rms(h_ref[...], ng_ref[...])
    fa = _mm(u, w_ref[...])
    rows = lax.broadcasted_iota(jnp.int32, (tt, 1), 0)
    prev = _shift_rows(fa, prev_ref[...], 1, rows)
    prev_ref[...] = fa[tt - CARRY_ROWS:, :]
    fa = fa + mu_ref[...] * (prev - fa)
    r, k, v = fa[:, :512], fa[:, 512:1024], fa[:, 1024:1536]
    lo = fa[:, 1536:1664]
    lane = lax.broadcasted_iota(jnp.int32, (1, LANES), 1)
    pre = _mm(jnp.where(lane < 64, jnp.tanh(lo), lo), lora_ref[...])
    lw = -RWKV_DECAY_SCALE * jax.nn.sigmoid(w0_ref[...] + pre[:, :512])
    a = jax.nn.sigmoid(a0_ref[...] + pre[:, 512:])
    g_s[...] = _mm(jax.nn.sigmoid(fa[:, 1664:1792]), gup_ref[...])
    bd = bd_ref[...]
    kk = k * kk_ref[...]
    kk = kk * lax.rsqrt(jnp.maximum(_sel_rhs(kk * kk, bd), 1e-24))
    k = k * (1.0 + (a - 1.0) * ka_ref[...])
    bon_s[...] = _sel_rhs(r * k * rk_ref[...], bd) * v
    b = _sel_lhs(tri_ref[...], lw)
    bl = _sel_lhs(blk_ref[...], lw)
    eb, enb, et = jnp.exp(b), jnp.exp(-b), jnp.exp(bl - b)
    beta = kk * a
    ab_s[...] = -kk * jnp.exp(b - lw)
    rb_s[...] = r * eb
    bt_s[...] = beta * enb
    kt_s[...] = k * enb
    v_s[...] = v
    bg_s[...] = beta * et
    kg_s[...] = k * et
    gl_s[...] = jnp.exp(bl)

    ri = lax.broadcasted_iota(jnp.int32, (2 * L, 2 * L), 0)
    ci = lax.broadcasted_iota(jnp.int32, (2 * L, 2 * L), 1)
    same = (ri // L) == (ci // L)
    strict = jnp.logical_and(same, (ri % L) > (ci % L))
    incl = jnp.logical_and(same, (ri % L) >= (ci % L))
    eye = jnp.where(ri == ci, 1.0, 0.0).astype(F32)
    m0 = lane < 64

    def stack(x):
        return jnp.concatenate([jnp.where(m0, x, 0.0), jnp.where(m0, 0.0, x)], axis=0)

    def body(c, carry):
        r0 = pl.multiple_of(c * L, L)
        rs = pl.ds(r0, L)
        for p in range(MIX // LANES):
            cs = slice(p * LANES, (p + 1) * LANES)
            xa, xr = stack(ab_s[rs, cs]), stack(rb_s[rs, cs])
            xb, xk = stack(bt_s[rs, cs]), stack(kt_s[rs, cs])
            xv = stack(v_s[rs, cs])
            xbg, xkg = stack(bg_s[rs, cs]), stack(kg_s[rs, cs])
            st = st_ref[p]
            gm = _mm_nt(jnp.concatenate([xa, xr], axis=0), jnp.concatenate([xb, xk], axis=0))
            a_ab = jnp.where(strict, gm[:2 * L, :2 * L], 0.0)
            a_ak = jnp.where(strict, gm[:2 * L, 2 * L:], 0.0)
            a_rb = jnp.where(incl, gm[2 * L:, :2 * L], 0.0)
            a_rk = jnp.where(incl, gm[2 * L:, 2 * L:], 0.0)
            tinv, pw = eye + a_ab, a_ab
            for _ in range(int(np.log2(L)) - 1):
                pw = _mm(pw, pw)
                tinv = tinv + _mm(tinv, pw)
            uu = _mm(tinv, _mm_nt(xa, st) + _mm(a_ak, xv))
            yy = _mm_nt(xr, st) + _mm(a_rb, uu) + _mm(a_rk, xv)
            y_s[rs, cs] = yy[:L] + yy[L:]
            st_ref[p] = st * gl_s[pl.ds(r0, 1), cs] + _mm_tn(
                jnp.concatenate([uu, xv], axis=0), jnp.concatenate([xbg, xkg], axis=0))
        return carry

    lax.fori_loop(0, tt // L, body, 0)

    y = y_s[...]
    yc = y - _sel_rhs(y, bd) * (1.0 / 64)
    var = _sel_rhs(yc * yc, bd) * (1.0 / 64)
    yn = yc * lax.rsqrt(var + RWKV_GN_EPS) * gng_ref[...] + gnb_ref[...]
    out_ref[...] = ((yn + bon_s[...]) * g_s[...]).astype(BF16)


def _gla_kernel(h_ref, ng_ref, w_ref, fup_ref, fb_ref, gn_ref, tri_ref, blk_ref, out_ref,
                st_ref, qd_s, kd_s, kg_s, v_s, gl_s, o_s, og_s, *, tt):
    L = GLA_CHUNK
    nh = 4

    @pl.when(pl.program_id(1) == 0)
    def _init():
        st_ref[...] = jnp.zeros_like(st_ref)

    u = _rms(h_ref[...], ng_ref[...])
    fb = _mm(u, w_ref[...])
    q, k, v = fb[:, :256] * 0.125, fb[:, 256:512], fb[:, 512:1024]
    og_s[...] = _silu(fb[:, 1024:1536])
    la = _log_sigmoid(_mm(fb[:, 1536:1664], fup_ref[...]) + fb_ref[...]) * (1.0 / 16.0)
    b = _sel_lhs(tri_ref[...], la)
    bl = _sel_lhs(blk_ref[...], la)
    qd_s[...] = q * jnp.exp(b)
    kd_s[...] = k * jnp.exp(-b)
    kg_s[...] = k * jnp.exp(bl - b)
    gl_s[...] = jnp.exp(bl)
    v_s[...] = v

    hq = lax.broadcasted_iota(jnp.int32, (1, 256), 1) // 64
    hv = lax.broadcasted_iota(jnp.int32, (1, 512), 1) // 128

    def stack(x, hid):
        return jnp.concatenate([jnp.where(hid == h, x, 0.0) for h in range(nh)], axis=0)

    ri = lax.broadcasted_iota(jnp.int32, (nh * L, nh * L), 0)
    ci = lax.broadcasted_iota(jnp.int32, (nh * L, nh * L), 1)
    amask = jnp.logical_and((ri // L) == (ci // L), (ri % L) >= (ci % L))

    def body(c, carry):
        r0 = pl.multiple_of(c * L, L)
        rs = pl.ds(r0, L)
        xq, xk, xkg = stack(qd_s[rs, :], hq), stack(kd_s[rs, :], hq), stack(kg_s[rs, :], hq)
        xv = stack(v_s[rs, :], hv)
        st = st_ref[...]
        att = jnp.where(amask, _mm_nt(xq, xk), 0.0)
        oo = _mm(att, xv) + _mm_nt(xq, st)
        acc = oo[:L]
        for h in range(1, nh):
            acc = acc + oo[h * L:(h + 1) * L]
        o_s[rs, :] = acc
        st_ref[...] = st * gl_s[pl.ds(r0, 1), :] + _mm_tn(xv, xkg)
        return carry

    lax.fori_loop(0, tt // L, body, 0)

    for h in range(nh):
        cs = slice(h * LANES, (h + 1) * LANES)
        oh = o_s[:, cs]
        oh = oh * lax.rsqrt(jnp.mean(oh * oh, axis=-1, keepdims=True) + EPS) * gn_ref[...]
        out_ref[:, cs] = (oh * og_s[:, cs]).astype(BF16)


def _mlstm_kernel(h_ref, ng_ref, w_ref, cw_ref, cb_ref, ib_ref, fbias_ref, gn_ref, tri_ref, out_ref,
                  c_ref, m_ref, prev_ref, q_s, k_s, v_s, b_s, li_s, h_s, og_s, *, tt):
    L = MLSTM_CHUNK
    nh = 4

    @pl.when(pl.program_id(1) == 0)
    def _init():
        c_ref[...] = jnp.zeros_like(c_ref)
        m_ref[...] = jnp.zeros_like(m_ref)
        prev_ref[...] = jnp.zeros_like(prev_ref)

    u = _rms(h_ref[...], ng_ref[...])
    fc = _mm(u, w_ref[...])
    rows = lax.broadcasted_iota(jnp.int32, (tt, 1), 0)
    qk = fc[:, :512]
    conv = _causal_conv(qk, prev_ref[...], cw_ref[...], cb_ref[...], rows)
    prev_ref[...] = qk[tt - CARRY_ROWS:, :]
    qk = _silu(conv)
    q_s[...] = qk[:, :256]
    k_s[...] = qk[:, 256:] * 0.125
    v_s[...] = fc[:, 512:1024]
    og_s[...] = jax.nn.sigmoid(fc[:, 1024:1536])
    misc = fc[:, 1536:1664]
    li_s[...] = misc + ib_ref[...]
    b_s[...] = _sel_lhs(tri_ref[...], _log_sigmoid(misc + fbias_ref[...]))

    hq = lax.broadcasted_iota(jnp.int32, (1, 256), 1) // 64
    lane = lax.broadcasted_iota(jnp.int32, (1, LANES), 1)
    ri = lax.broadcasted_iota(jnp.int32, (nh * L, nh * L), 0)
    ci = lax.broadcasted_iota(jnp.int32, (nh * L, nh * L), 1)
    dmask = jnp.logical_and((ri // L) == (ci // L), (ri % L) >= (ci % L))
    ones = jnp.ones((L, LANES), F32)

    def stackq(x):
        return jnp.concatenate([jnp.where(hq == h, x, 0.0) for h in range(nh)], axis=0)

    def percol(x, off):
        return jnp.concatenate([x[:, off + h:off + h + 1] for h in range(nh)], axis=0)

    def perhead_last(col):
        return jnp.concatenate(
            [jnp.broadcast_to(col[(h + 1) * L - 1:(h + 1) * L, :], (L, 1)) for h in range(nh)], axis=0)

    def body(c, carry):
        rs = pl.ds(pl.multiple_of(c * L, L), L)
        xq, xk = stackq(q_s[rs, :]), stackq(k_s[rs, :])
        vc = v_s[rs, :]
        xv = jnp.concatenate(
            [jnp.concatenate([vc[:, h * LANES:(h + 1) * LANES], ones], axis=1) for h in range(nh)], axis=0)
        bcol, licol = percol(b_s[rs, :], nh), percol(li_s[rs, :], 0)
        zt = jnp.where(lane == 0, bcol, jnp.where(lane == 1, licol, 0.0)).T
        brow, lirow = zt[0:1, :], zt[1:2, :]
        mcol = m_ref[:, 0:1]
        log_d = jnp.where(dmask, bcol - brow + lirow, -jnp.inf)
        m_t = jnp.maximum(bcol + mcol, jnp.max(log_d, axis=-1, keepdims=True))
        d = jnp.exp(log_d - m_t)
        inter = jnp.exp(bcol + mcol - m_t)
        sc = _mm_nt(xq, xk) * d
        num = _mm(sc, xv) + _mm(xq, c_ref[...]) * inter
        hh = num[:, :LANES] / jnp.maximum(jnp.abs(num[:, LANES:]), jnp.exp(-m_t))
        for h in range(nh):
            h_s[rs, h * LANES:(h + 1) * LANES] = hh[h * L:(h + 1) * L, :]
        m_new = perhead_last(m_t)
        w_last = jnp.exp(perhead_last(bcol) - bcol + licol - m_new)
        c_ref[...] = perhead_last(inter) * c_ref[...] + _mm_tn(xk * w_last, xv)
        m_ref[...] = jnp.broadcast_to(m_new, m_ref.shape)
        return carry

    lax.fori_loop(0, tt // L, body, 0)

    for h in range(nh):
        cs = slice(h * LANES, (h + 1) * LANES)
        hh = h_s[:, cs]
        hh = hh * lax.rsqrt(jnp.mean(hh * hh, axis=-1, keepdims=True) + EPS) * gn_ref[:, cs]
        out_ref[:, cs] = (hh * og_s[:, cs]).astype(BF16)


def _ssd_kernel(h_ref, ng_ref, w_ref, cw_ref, cb_ref, dtb_ref, alog_ref, dsk_ref, gn_ref, tri_ref,
                ex_ref, out_ref, st_ref, prev_ref, x_s, xd_s, bm_s, cm_s, ac_s, ae_s, y_s, z_s, *, tt):
    L = SSD_CHUNK

    @pl.when(pl.program_id(1) == 0)
    def _init():
        st_ref[...] = jnp.zeros_like(st_ref)
        prev_ref[...] = jnp.zeros_like(prev_ref)

    u = _rms(h_ref[...], ng_ref[...])
    fd = _mm(u, w_ref[...])
    rows = lax.broadcasted_iota(jnp.int32, (tt, 1), 0)
    z_s[...] = _silu(fd[:, :512])
    xbc = fd[:, 512:1536]
    conv = _silu(_causal_conv(xbc, prev_ref[...], cw_ref[...], cb_ref[...], rows))
    prev_ref[...] = xbc[tt - CARRY_ROWS:, :]
    x = conv[:, :512]
    x_s[...] = x
    bm_s[...] = conv[:, 512:768]
    cm_s[...] = conv[:, 768:1024]
    lane = lax.broadcasted_iota(jnp.int32, (1, LANES), 1)
    valid = lane < 8
    dt = jnp.where(valid, _softplus(fd[:, 1536:1664] + dtb_ref[...]), 0.0)
    dta = dt * jnp.where(valid, -jnp.exp(alog_ref[...]), 0.0)
    ac = _sel_lhs(tri_ref[...], dta)
    ac_s[...] = ac
    ae_s[...] = _sel_rhs(ac, ex_ref[...], 3)
    xd_s[...] = x * _sel_rhs(dt, ex_ref[...], 3)

    ri = lax.broadcasted_iota(jnp.int32, (L, L), 0)
    ci = lax.broadcasted_iota(jnp.int32, (L, L), 1)
    tril = ri >= ci
    m0 = lane < 64

    def body(c, carry):
        rs = pl.ds(pl.multiple_of(c * L, L), L)
        acc = ac_s[rs, :]
        act = acc.T
        ace = ae_s[rs, :]
        last = ace[L - 1:L, :]
        ea, dec, gl = jnp.exp(ace), jnp.exp(last - ace), jnp.exp(last)
        xd = xd_s[rs, :]
        xdec = xd * dec
        for g in range(2):
            gs = slice(g * LANES, (g + 1) * LANES)
            bg, cg = bm_s[rs, gs], cm_s[rs, gs]
            cb = _mm_nt(cg, bg)
            for pr in range(2):
                p = 2 * g + pr
                cs = slice(p * LANES, (p + 1) * LANES)
                xp = xd[:, cs]
                y = _mm(cg, st_ref[p]) * ea[:, cs]
                for hh, xm in ((2 * p, jnp.where(m0, xp, 0.0)), (2 * p + 1, jnp.where(m0, 0.0, xp))):
                    seg = jnp.exp(jnp.where(tril, acc[:, hh:hh + 1] - act[hh:hh + 1, :], -jnp.inf))
                    y = y + _mm(cb * seg, xm)
                y_s[rs, cs] = y
                st_ref[p] = st_ref[p] * gl[:, cs] + _mm_tn(bg, xdec[:, cs])
        return carry

    lax.fori_loop(0, tt // L, body, 0)

    y = (y_s[...] + x_s[...] * dsk_ref[...]) * z_s[...]
    out_ref[...] = _rms(y, gn_ref[...]).astype(BF16)


def _merge_kernel(h_ref, ng_ref, wg_ref, b0_ref, b1_ref, b2_ref, b3_ref, p_ref, wo_ref, out_ref, u_s, acc_s):
    j = pl.program_id(1)

    @pl.when(j == 0)
    def _init():
        u_s[...] = _rms(h_ref[...], ng_ref[...]).astype(BF16)
        acc_s[...] = jnp.zeros_like(acc_s)

    gate = jax.nn.sigmoid(jnp.dot(u_s[...], wg_ref[...], preferred_element_type=F32))
    for i, br in enumerate((b0_ref, b1_ref, b2_ref, b3_ref)):
        @pl.when(j == i)
        def _acc(br=br):
            acc_s[...] += gate * jnp.dot(br[...], p_ref[...], preferred_element_type=F32)

    @pl.when(j == N_BRANCH - 1)
    def _out():
        out_ref[...] = h_ref[...] + _mm(acc_s[...], wo_ref[...])


def _ffn_kernel(h_ref, ng_ref, upg_ref, upv_ref, cwg_ref, cwv_ref, cbg_ref, cbv_ref, dn_ref, out_ref,
                u_s, acc_s, cg_s, cv_s, *, tm, tiles_per_seq):
    i, j = pl.program_id(0), pl.program_id(1)

    @pl.when(j == 0)
    def _init():
        u_s[...] = _rms(h_ref[...], ng_ref[...]).astype(BF16)
        acc_s[...] = jnp.zeros_like(acc_s)

    seq_start = i % tiles_per_seq == 0
    rows = lax.broadcasted_iota(jnp.int32, (tm, 1), 0)

    def half(up_ref, cw_ref, cb_ref, carry_ref):
        z = jnp.dot(u_s[...], up_ref[...], preferred_element_type=F32)
        carry = jnp.where(seq_start, 0.0, carry_ref[j])
        carry_ref[j] = z[tm - CARRY_ROWS:, :]
        return _causal_conv(z, carry, cw_ref[...], cb_ref[...], rows)

    gate = half(upg_ref, cwg_ref, cbg_ref, cg_s)
    val = half(upv_ref, cwv_ref, cbv_ref, cv_s)
    acc_s[...] += _mm(_silu(gate) * val, dn_ref[...])

    @pl.when(j == pl.num_programs(1) - 1)
    def _out():
        out_ref[...] = h_ref[...] + acc_s[...]


def _norm_kernel(h_ref, g_ref, out_ref):
    out_ref[...] = _rms(h_ref[...], g_ref[...])


def _layer_spec(arr, l):
    tail = arr.shape[1:]
    return pl.BlockSpec((None,) + tail, lambda *_, l=l, n=len(tail): (l,) + (0,) * n)


def _full_spec(arr):
    return pl.BlockSpec(arr.shape, lambda *_, n=arr.ndim: (0,) * n)


def _mixer_call(kern, name, h, l, layer_params, consts, scratch, tt):
    bsz, t, d = h.shape
    in_specs = [pl.BlockSpec((None, tt, d), lambda b, i: (b, i, 0))]
    in_specs += [_layer_spec(a, l) for a in layer_params]
    in_specs += [_full_spec(a) for a in consts]
    return pl.pallas_call(
        functools.partial(kern, tt=tt),
        grid=(bsz, t // tt),
        in_specs=in_specs,
        out_specs=pl.BlockSpec((None, tt, MIX), lambda b, i: (b, i, 0)),
        out_shape=jax.ShapeDtypeStruct((bsz, t, MIX), BF16),
        scratch_shapes=scratch,
        compiler_params=pltpu.CompilerParams(dimension_semantics=("parallel", "arbitrary"),
                                             vmem_limit_bytes=VMEM_LIMIT),
        name=name,
    )(h, *layer_params, *consts)


def _vm(shape, dtype=F32):
    return pltpu.VMEM(shape, dtype)


def _block_diag_ones(n, blk, lower=False):
    i = np.arange(n)
    m = (i[:, None] // blk) == (i[None, :] // blk)
    if lower:
        m = m & (i[:, None] >= i[None, :])
    return jnp.asarray(m, BF16)


def _pad_lanes(a, width=LANES):
    return jnp.pad(a, [(0, 0)] * (a.ndim - 1) + [(0, width - a.shape[-1])])


def _row(a):
    return a[:, None, :]


def kernel(x, mix_norm_g, w_in, rwkv_mu, rwkv_w0, rwkv_w_up, rwkv_a0, rwkv_a_up, rwkv_g_up, rwkv_k_k, rwkv_k_a, rwkv_r_k, rwkv_gn_g, rwkv_gn_b, gla_f_up, gla_f_bias, gla_norm_g, mlstm_conv_w, mlstm_conv_b, mlstm_i_bias, mlstm_f_bias, mlstm_norm_g, ssd_conv_w, ssd_conv_b, ssd_dt_bias, ssd_a_log, ssd_d, ssd_norm_g, branch_proj, w_out, ffn_norm_g, ffn_up, ffn_conv_w, ffn_conv_b, ffn_down, final_norm_g):
    bsz, t, d = x.shape
    depth = w_in.shape[0]
    tt = MIXER_TILE
    assert d == D_MODEL and t % tt == 0

    def cols(lo, hi):
        return w_in[:, :, lo:hi]

    zpad = lambda n: jnp.zeros((depth, d, n), F32)
    w_rwkv = cols(0, RWKV_IN).astype(BF16)
    o = OFF_GLA
    w_gla = jnp.concatenate([cols(o, o + 1024), cols(o + 1040, o + 1552), cols(o + 1024, o + 1040),
                             zpad(LANES - 16)], axis=-1).astype(BF16)
    o = OFF_MLSTM
    w_mlstm = jnp.concatenate([cols(o, o + 1024), cols(o + 1032, o + 1544), cols(o + 1024, o + 1032),
                               zpad(LANES - 8)], axis=-1).astype(BF16)
    o = OFF_SSD
    w_ssd = jnp.concatenate([cols(o, o + 1544), zpad(LANES - 8)], axis=-1).astype(BF16)
    w_gate = cols(OFF_GATE, OFF_GATE + N_BRANCH * d).reshape(depth, d, N_BRANCH, d)
    w_gate = jnp.transpose(w_gate, (0, 2, 1, 3)).astype(BF16)

    ng = _row(mix_norm_g)
    zl = jnp.zeros_like(rwkv_w_up)
    rwkv_lora = jnp.concatenate([jnp.concatenate([rwkv_w_up, zl], axis=-1),
                                 jnp.concatenate([zl, rwkv_a_up], axis=-1)], axis=1).astype(BF16)
    rwkv_params = [ng, w_rwkv, _row(rwkv_mu), _row(rwkv_w0), _row(rwkv_a0), rwkv_lora,
                   rwkv_g_up.astype(BF16), _row(rwkv_k_k), _row(rwkv_k_a),
                   _row(rwkv_r_k.reshape(depth, MIX)), _row(rwkv_gn_g), _row(rwkv_gn_b)]
    rwkv_consts = [_block_diag_ones(MIX, 64), _block_diag_ones(tt, RWKV_CHUNK, lower=True),
                   _block_diag_ones(tt, RWKV_CHUNK)]
    rwkv_scratch = [_vm((MIX // LANES, LANES, LANES)), _vm((CARRY_ROWS, RWKV_IN))] + [_vm((tt, MIX))] * 11

    gla_fup = jnp.pad(gla_f_up, ((0, 0), (0, LANES - gla_f_up.shape[1]), (0, 0))).astype(BF16)
    gla_params = [ng, w_gla, gla_fup, _row(gla_f_bias), _row(gla_norm_g)]
    gla_consts = [_block_diag_ones(tt, GLA_CHUNK, lower=True), _block_diag_ones(tt, GLA_CHUNK)]
    gla_scratch = [_vm((MIX, 256)), _vm((tt, 256)), _vm((tt, 256)), _vm((tt, 256)), _vm((tt, MIX)),
                   _vm((tt, 256)), _vm((tt, MIX)), _vm((tt, MIX))]

    zh = jnp.zeros_like(mlstm_i_bias)
    mlstm_params = [ng, w_mlstm, mlstm_conv_w, _row(mlstm_conv_b),
                    _row(_pad_lanes(jnp.concatenate([mlstm_i_bias, zh], axis=-1))),
                    _row(_pad_lanes(jnp.concatenate([zh, mlstm_f_bias], axis=-1))),
                    _row(mlstm_norm_g)]
    mlstm_consts = [_block_diag_ones(tt, MLSTM_CHUNK, lower=True)]
    mlstm_scratch = [_vm((256, 256)), _vm((256, LANES)), _vm((CARRY_ROWS, 512)),
                     _vm((tt, 256)), _vm((tt, 256)), _vm((tt, MIX)), _vm((tt, LANES)), _vm((tt, LANES)),
                     _vm((tt, MIX)), _vm((tt, MIX))]

    ssd_params = [ng, w_ssd, ssd_conv_w, _row(ssd_conv_b), _row(_pad_lanes(ssd_dt_bias)),
                  _row(_pad_lanes(ssd_a_log)), _row(jnp.repeat(ssd_d, 64, axis=-1)), _row(ssd_norm_g)]
    hid = np.arange(MIX) // 64
    expand = jnp.asarray(np.arange(LANES)[:, None] == hid[None, :], BF16)
    ssd_consts = [_block_diag_ones(tt, SSD_CHUNK, lower=True), expand]
    ssd_scratch = [_vm((MIX // LANES, LANES, LANES)), _vm((CARRY_ROWS, 1024)),
                   _vm((tt, MIX)), _vm((tt, MIX)), _vm((tt, 256)), _vm((tt, 256)), _vm((tt, LANES)),
                   _vm((tt, MIX)), _vm((tt, MIX)), _vm((tt, MIX))]

    bproj = branch_proj.astype(BF16)
    wout = w_out.astype(BF16)
    fng = _row(ffn_norm_g)
    fup = ffn_up.astype(BF16)
    fdn = ffn_down.astype(BF16)
    fcb = _row(ffn_conv_b)

    m = bsz * t
    tm = 512
    tf = D_FF // 2
    nf = D_FF // tf
    assert t % tm == 0
    cparams = pltpu.CompilerParams(dimension_semantics=("parallel", "arbitrary"), vmem_limit_bytes=VMEM_LIMIT)

    def merge(h2, l, branches):
        row_spec = pl.BlockSpec((tm, d), lambda i, j: (i, 0))
        br_spec = pl.BlockSpec((tm, MIX), lambda i, j: (i, 0))
        return pl.pallas_call(
            _merge_kernel,
            grid=(m // tm, N_BRANCH),
            in_specs=[row_spec, _layer_spec(ng, l),
                      pl.BlockSpec((None, None, d, d), lambda i, j: (l, j, 0, 0)),
                      br_spec, br_spec, br_spec, br_spec,
                      pl.BlockSpec((None, None, MIX, d), lambda i, j: (l, j, 0, 0)),
                      _layer_spec(wout, l)],
            out_specs=row_spec,
            out_shape=jax.ShapeDtypeStruct((m, d), F32),
            scratch_shapes=[_vm((tm, d), BF16), _vm((tm, d))],
            compiler_params=cparams,
            name="merge",
        )(h2, ng, w_gate, *branches, bproj, wout)

    def ffn(h2, l):
        row_spec = pl.BlockSpec((tm, d), lambda i, j: (i, 0))
        return pl.pallas_call(
            functools.partial(_ffn_kernel, tm=tm, tiles_per_seq=t // tm),
            grid=(m // tm, nf),
            in_specs=[row_spec, _layer_spec(fng, l),
                      pl.BlockSpec((None, d, tf), lambda i, j: (l, 0, j)),
                      pl.BlockSpec((None, d, tf), lambda i, j: (l, 0, nf + j)),
                      pl.BlockSpec((None, 3, tf), lambda i, j: (l, 0, j)),
                      pl.BlockSpec((None, 3, tf), lambda i, j: (l, 0, nf + j)),
                      pl.BlockSpec((None, 1, tf), lambda i, j: (l, 0, j)),
                      pl.BlockSpec((None, 1, tf), lambda i, j: (l, 0, nf + j)),
                      pl.BlockSpec((None, tf, d), lambda i, j: (l, j, 0))],
            out_specs=row_spec,
            out_shape=jax.ShapeDtypeStruct((m, d), F32),
            scratch_shapes=[_vm((tm, d), BF16), _vm((tm, d)), _vm((nf, CARRY_ROWS, tf)),
                            _vm((nf, CARRY_ROWS, tf))],
            compiler_params=cparams,
            name="ffn",
        )(h2, fng, fup, fup, ffn_conv_w, ffn_conv_w, fcb, fcb, fdn)

    h = x
    for l in range(depth):
        branches = [
            _mixer_call(_rwkv_kernel, "rwkv", h, l, rwkv_params, rwkv_consts, rwkv_scratch, tt),
            _mixer_call(_gla_kernel, "gla", h, l, gla_params, gla_consts, gla_scratch, tt),
            _mixer_call(_mlstm_kernel, "mlstm", h, l, mlstm_params, mlstm_consts, mlstm_scratch, tt),
            _mixer_call(_ssd_kernel, "ssd", h, l, ssd_params, ssd_consts, ssd_scratch, tt),
        ]
        h2 = merge(h.reshape(m, d), l, [b.reshape(m, MIX) for b in branches])
        h = ffn(h2, l).reshape(bsz, t, d)

    out = pl.pallas_call(
        _norm_kernel,
        grid=(m // tm,),
        in_specs=[pl.BlockSpec((tm, d), lambda i: (i, 0)), pl.BlockSpec((1, d), lambda i: (0, 0))],
        out_specs=pl.BlockSpec((tm, d), lambda i: (i, 0)),
        out_shape=jax.ShapeDtypeStruct((m, d), x.dtype),
        name="final_norm",
    )(h.reshape(m, d), final_norm_g[None, :])
    return out.reshape(bsz, t, d)
```

```python
import functools

import numpy as np
import jax
import jax.numpy as jnp
from jax import lax
from jax.experimental import pallas as pl
from jax.experimental.pallas import tpu as pltpu

F32, BF16 = jnp.float32, jnp.bfloat16

D_MODEL = 1024
MIX = 512
EPS = 1e-6
N_BRANCH = 4
RWKV_DECAY_SCALE = 0.6065306597
RWKV_GN_EPS = 64e-5
RWKV_CHUNK = 64
GLA_CHUNK = 16
MLSTM_CHUNK = 64
SSD_CHUNK = 128
D_FF = 2816

RWKV_IN, GLA_IN, MLSTM_IN, SSD_IN = 1792, 1552, 1544, 1544
OFF_GLA = RWKV_IN
OFF_MLSTM = OFF_GLA + GLA_IN
OFF_SSD = OFF_MLSTM + MLSTM_IN
OFF_GATE = OFF_SSD + SSD_IN

LANES = 128
CARRY_ROWS = 8
MIXER_TILE = 128
VMEM_LIMIT = 48 * 1024 * 1024


def _mm(a, b):
    return jnp.dot(a.astype(BF16), b.astype(BF16), preferred_element_type=F32)


def _mm_nt(a, b):
    return lax.dot_general(a.astype(BF16), b.astype(BF16), (((1,), (1,)), ((), ())),
                           preferred_element_type=F32)


def _mm_tn(a, b):
    return lax.dot_general(a.astype(BF16), b.astype(BF16), (((0,), (0,)), ((), ())),
                           preferred_element_type=F32)


def _bf16_terms(x, n):
    terms, rest = [], x
    for _ in range(n):
        t = rest.astype(BF16)
        terms.append(t)
        rest = rest - t.astype(F32)
    return terms


def _sel_lhs(sel, x, n=3):
    out = None
    for t in _bf16_terms(x, n):
        p = jnp.dot(sel, t, preferred_element_type=F32)
        out = p if out is None else out + p
    return out


def _sel_rhs(x, sel, n=2):
    out = None
    for t in _bf16_terms(x, n):
        p = jnp.dot(t, sel, preferred_element_type=F32)
        out = p if out is None else out + p
    return out


def _chunk_sums(sel, x, nb, tt, n=2):
    parts = [_sel_lhs(sel, x[b * tt:(b + 1) * tt], n) for b in range(nb)]
    return (jnp.concatenate([p[:tt] for p in parts], axis=0),
            jnp.concatenate([p[tt:] for p in parts], axis=0))


def _rms(x, g):
    return x * lax.rsqrt(jnp.mean(x * x, axis=-1, keepdims=True) + EPS) * g


def _silu(x):
    return x * jax.nn.sigmoid(x)


def _softplus(x):
    return jnp.maximum(x, 0.0) + jnp.log1p(jnp.exp(-jnp.abs(x)))


def _log_sigmoid(x):
    return -_softplus(-x)


def _shift_rows(x, carry, s):
    y = pltpu.roll(x, s, 0)
    rows = lax.broadcasted_iota(jnp.int32, (CARRY_ROWS, 1), 0)
    head = y[:CARRY_ROWS]
    for q in range(s):
        src = CARRY_ROWS - s + q
        head = jnp.where(rows == q, carry[src:src + 1, :], head)
    return jnp.concatenate([head, y[CARRY_ROWS:]], axis=0)


def _causal_conv(x, carries, w, b, tt):
    k = w.shape[0]
    y = w[k - 1:k, :] * x + b
    for s in range(1, k):
        sh = jnp.concatenate([_shift_rows(x[i * tt:(i + 1) * tt], c, s) for i, c in enumerate(carries)],
                             axis=0)
        y = y + w[k - 1 - s:k - s, :] * sh
    return y


def _save_carry(carry_ref, x, nb, tt):
    for b in range(nb):
        carry_ref[b] = x[(b + 1) * tt - CARRY_ROWS:(b + 1) * tt, :]


def _first_step_zero(*refs):
    @pl.when(pl.program_id(0) == 0)
    def _init():
        for r in refs:
            r[...] = jnp.zeros_like(r)


def _rwkv_kernel(h_ref, ng_ref, w_ref, mu_ref, w0_ref, a0_ref, lora_ref, gup_ref, kk_ref, ka_ref,
                 rk_ref, gng_ref, gnb_ref, bd_ref, sel_ref, out_ref,
                 st_ref, prev_ref, ab_s, rb_s, bt_s, kt_s, v_s, bg_s, kg_s, gl_s, y_s, bon_s, g_s,
                 *, nb, tt):
    L = RWKV_CHUNK
    npair = MIX // LANES
    _first_step_zero(st_ref, prev_ref)

    u = _rms(h_ref[...].reshape(nb * tt, D_MODEL), ng_ref[...])
    fa = _mm(u, w_ref[...])
    prev = jnp.concatenate([_shift_rows(fa[b * tt:(b + 1) * tt], prev_ref[b], 1) for b in range(nb)], axis=0)
    _save_carry(prev_ref, fa, nb, tt)
    fa = fa + mu_ref[...] * (prev - fa)
    r, k, v = fa[:, :512], fa[:, 512:1024], fa[:, 1024:1536]
    lo = fa[:, 1536:1664]
    lane = lax.broadcasted_iota(jnp.int32, (1, LANES), 1)
    pre = _mm(jnp.where(lane < 64, jnp.tanh(lo), lo), lora_ref[...])
    lw = -RWKV_DECAY_SCALE * jax.nn.sigmoid(w0_ref[...] + pre[:, :512])
    a = jax.nn.sigmoid(a0_ref[...] + pre[:, 512:])
    g_s[...] = _mm(jax.nn.sigmoid(fa[:, 1664:1792]), gup_ref[...])
    bd = bd_ref[...]
    kk = k * kk_ref[...]
    kk = kk * lax.rsqrt(jnp.maximum(_sel_rhs(kk * kk, bd, 1), 1e-24))
    k = k * (1.0 + (a - 1.0) * ka_ref[...])
    bon_s[...] = _sel_rhs(r * k * rk_ref[...], bd, 1) * v
    b, bl = _chunk_sums(sel_ref[...], lw, nb, tt)
    eb, enb, et = jnp.exp(b), jnp.exp(-b), jnp.exp(bl - b)
    beta = kk * a
    ab_s[...] = (-kk * jnp.exp(b - lw)).astype(BF16)
    rb_s[...] = (r * eb).astype(BF16)
    bt_s[...] = (beta * enb).astype(BF16)
    kt_s[...] = (k * enb).astype(BF16)
    v_s[...] = v.astype(BF16)
    bg_s[...] = (beta * et).astype(BF16)
    kg_s[...] = (k * et).astype(BF16)
    gl_s[...] = jnp.exp(bl)

    ri = lax.broadcasted_iota(jnp.int32, (2 * L, 2 * L), 0)
    ci = lax.broadcasted_iota(jnp.int32, (2 * L, 2 * L), 1)
    same = (ri // L) == (ci // L)
    strict = jnp.logical_and(same, (ri % L) > (ci % L))
    incl = jnp.logical_and(same, (ri % L) >= (ci % L))
    eye = jnp.where(ri == ci, 1.0, 0.0).astype(F32)
    m0 = lane < 64
    chains = [(bi, p) for bi in range(nb) for p in range(npair)]

    def body(c, carry):
        r0 = pl.multiple_of(c * L, L)

        def ld(ref, bi, p):
            x = ref[pl.ds(bi * tt + r0, L), p * LANES:(p + 1) * LANES]
            zero = jnp.zeros_like(x)
            return jnp.concatenate([jnp.where(m0, x, zero), jnp.where(m0, zero, x)], axis=0)

        def ld2(ref_a, ref_b, bi, p):
            return jnp.concatenate([ld(ref_a, bi, p), ld(ref_b, bi, p)], axis=0)

        gm = [_mm_nt(ld2(ab_s, rb_s, bi, p), ld2(bt_s, kt_s, bi, p)) for bi, p in chains]
        a_ab = [jnp.where(strict, x[:2 * L, :2 * L], 0.0) for x in gm]
        a_ak = [jnp.where(strict, x[:2 * L, 2 * L:], 0.0).astype(BF16) for x in gm]
        a_rb = [jnp.where(incl, x[2 * L:, :2 * L], 0.0).astype(BF16) for x in gm]
        a_rk = [jnp.where(incl, x[2 * L:, 2 * L:], 0.0).astype(BF16) for x in gm]
        tinv = [eye + x for x in a_ab]
        pw = [x.astype(BF16) for x in a_ab]
        for _ in range(int(np.log2(L)) - 1):
            pw = [_mm(x, x).astype(BF16) for x in pw]
            tinv = [t + _mm(t, x) for t, x in zip(tinv, pw)]
        st = [st_ref[bi, p] for bi, p in chains]
        stb = [x.astype(BF16) for x in st]
        w1 = [_mm_nt(ld(ab_s, bi, p), s) + _mm(ak, ld(v_s, bi, p))
              for (bi, p), s, ak in zip(chains, stb, a_ak)]
        uu = [_mm(t, x).astype(BF16) for t, x in zip(tinv, w1)]
        yy = [_mm_nt(ld(rb_s, bi, p), s) + _mm(rb, x) + _mm(rk, ld(v_s, bi, p))
              for (bi, p), s, rb, rk, x in zip(chains, stb, a_rb, a_rk, uu)]
        for (bi, p), y in zip(chains, yy):
            y_s[pl.ds(bi * tt + r0, L), p * LANES:(p + 1) * LANES] = y[:L] + y[L:]
        upd = [_mm_tn(jnp.concatenate([x, ld(v_s, bi, p)], axis=0), ld2(bg_s, kg_s, bi, p))
               for (bi, p), x in zip(chains, uu)]
        for (bi, p), s, d in zip(chains, st, upd):
            st_ref[bi, p] = s * gl_s[pl.ds(bi * tt + r0, 1), p * LANES:(p + 1) * LANES] + d
        return carry

    lax.fori_loop(0, tt // L, body, 0)

    y = y_s[...]
    yc = y - _sel_rhs(y, bd, 1) * (1.0 / 64)
    var = _sel_rhs(yc * yc, bd, 1) * (1.0 / 64)
    yn = yc * lax.rsqrt(var + RWKV_GN_EPS) * gng_ref[...] + gnb_ref[...]
    out_ref[...] = ((yn + bon_s[...]) * g_s[...]).astype(BF16).reshape(nb, tt, MIX)


def _gla_kernel(h_ref, ng_ref, w_ref, fup_ref, fb_ref, gn_ref, sel_ref, out_ref,
                st_ref, qd_s, kd_s, kg_s, v_s, gl_s, o_s, og_s, *, nb, tt):
    L = GLA_CHUNK
    nh = 4
    _first_step_zero(st_ref)

    u = _rms(h_ref[...].reshape(nb * tt, D_MODEL), ng_ref[...])
    fb = _mm(u, w_ref[...])
    q, k = fb[:, :256] * 0.125, fb[:, 256:512]
    v_s[...] = fb[:, 512:1024].astype(BF16)
    og_s[...] = _silu(fb[:, 1024:1536])
    la = _log_sigmoid(_mm(fb[:, 1536:1664], fup_ref[...]) + fb_ref[...]) * (1.0 / 16.0)
    b, bl = _chunk_sums(sel_ref[...], la, nb, tt)
    qd_s[...] = (q * jnp.exp(b)).astype(BF16)
    kd_s[...] = (k * jnp.exp(-b)).astype(BF16)
    kg_s[...] = (k * jnp.exp(bl - b)).astype(BF16)
    gl_s[...] = jnp.exp(bl)

    hq = lax.broadcasted_iota(jnp.int32, (1, 256), 1) // 64
    hv = lax.broadcasted_iota(jnp.int32, (1, 512), 1) // 128
    ri = lax.broadcasted_iota(jnp.int32, (nh * L, nh * L), 0)
    ci = lax.broadcasted_iota(jnp.int32, (nh * L, nh * L), 1)
    amask = jnp.logical_and((ri // L) == (ci // L), (ri % L) >= (ci % L))
    seqs = range(nb)

    def body(c, carry):
        r0 = pl.multiple_of(c * L, L)

        def ld(ref, bi, hid):
            x = ref[pl.ds(bi * tt + r0, L), :]
            zero = jnp.zeros_like(x)
            return jnp.concatenate([jnp.where(hid == h, x, zero) for h in range(nh)], axis=0)

        xq = [ld(qd_s, bi, hq) for bi in seqs]
        xv = [ld(v_s, bi, hv) for bi in seqs]
        att = [jnp.where(amask, _mm_nt(xq[bi], ld(kd_s, bi, hq)), 0.0) for bi in seqs]
        upd = [_mm_tn(xv[bi], ld(kg_s, bi, hq)) for bi in seqs]
        st = [st_ref[bi] for bi in seqs]
        oi = [_mm_nt(xq[bi], st[bi]) for bi in seqs]
        oo = [_mm(att[bi], xv[bi]) + oi[bi] for bi in seqs]
        for bi in seqs:
            acc = oo[bi][:L]
            for h in range(1, nh):
                acc = acc + oo[bi][h * L:(h + 1) * L]
            o_s[pl.ds(bi * tt + r0, L), :] = acc
            st_ref[bi] = st[bi] * gl_s[pl.ds(bi * tt + r0, 1), :] + upd[bi]
        return carry

    lax.fori_loop(0, tt // L, body, 0)

    for h in range(nh):
        cs = slice(h * LANES, (h + 1) * LANES)
        oh = o_s[:, cs]
        oh = oh * lax.rsqrt(jnp.mean(oh * oh, axis=-1, keepdims=True) + EPS) * gn_ref[...]
        out_ref[:, :, cs] = (oh * og_s[:, cs]).astype(BF16).reshape(nb, tt, LANES)


def _mlstm_kernel(h_ref, ng_ref, w_ref, cw_ref, cb_ref, ib_ref, fbias_ref, gn_ref, sel_ref, out_ref,
                  c_ref, m_ref, prev_ref, q_s, k_s, v_s, b_s, li_s, h_s, og_s, *, nb, tt):
    L = MLSTM_CHUNK
    nh = 4
    _first_step_zero(c_ref, m_ref, prev_ref)

    u = _rms(h_ref[...].reshape(nb * tt, D_MODEL), ng_ref[...])
    fc = _mm(u, w_ref[...])
    qk = fc[:, :512]
    conv = _causal_conv(qk, [prev_ref[b] for b in range(nb)], cw_ref[...], cb_ref[...], tt)
    _save_carry(prev_ref, qk, nb, tt)
    qk = _silu(conv)
    q_s[...] = qk[:, :256].astype(BF16)
    k_s[...] = qk[:, 256:] * 0.125
    v_s[...] = fc[:, 512:1024].astype(BF16)
    og_s[...] = jax.nn.sigmoid(fc[:, 1024:1536])
    misc = fc[:, 1536:1664]
    li_s[...] = misc + ib_ref[...]
    b_s[...] = _chunk_sums(sel_ref[...], _log_sigmoid(misc + fbias_ref[...]), nb, tt)[0]

    hq = lax.broadcasted_iota(jnp.int32, (1, 256), 1) // 64
    lane = lax.broadcasted_iota(jnp.int32, (1, LANES), 1)
    ri = lax.broadcasted_iota(jnp.int32, (nh * L, nh * L), 0)
    ci = lax.broadcasted_iota(jnp.int32, (nh * L, nh * L), 1)
    dmask = jnp.logical_and((ri // L) == (ci // L), (ri % L) >= (ci % L))
    ones = jnp.ones((L, LANES), BF16)
    seqs = range(nb)

    def stackq(x):
        zero = jnp.zeros_like(x)
        return jnp.concatenate([jnp.where(hq == h, x, zero) for h in range(nh)], axis=0)

    def percol(x, off):
        return jnp.concatenate([x[:, off + h:off + h + 1] for h in range(nh)], axis=0)

    def perhead_last(col):
        return jnp.concatenate(
            [jnp.broadcast_to(col[(h + 1) * L - 1:(h + 1) * L, :], (L, 1)) for h in range(nh)], axis=0)

    def body(c, carry):
        r0 = pl.multiple_of(c * L, L)
        rows = [pl.ds(bi * tt + r0, L) for bi in seqs]
        xq = [stackq(q_s[rs, :]) for rs in rows]
        kf = [stackq(k_s[rs, :]) for rs in rows]
        xv = []
        for rs in rows:
            vc = v_s[rs, :]
            xv.append(jnp.concatenate(
                [jnp.concatenate([vc[:, h * LANES:(h + 1) * LANES], ones], axis=1) for h in range(nh)], axis=0))
        qk = [_mm_nt(xq[bi], kf[bi]) for bi in seqs]
        qc = [_mm(xq[bi], c_ref[bi]) for bi in seqs]
        for bi in seqs:
            rs = rows[bi]
            bcol, licol = percol(b_s[rs, :], nh), percol(li_s[rs, :], 0)
            zt = jnp.where(lane == 0, bcol, jnp.where(lane == 1, licol, 0.0)).T
            brow, lirow = zt[0:1, :], zt[1:2, :]
            mcol = m_ref[bi, :, 0:1]
            log_d = jnp.where(dmask, bcol - brow + lirow, -jnp.inf)
            m_t = jnp.maximum(bcol + mcol, jnp.max(log_d, axis=-1, keepdims=True))
            d = jnp.exp(log_d - m_t)
            inter = jnp.exp(bcol + mcol - m_t)
            num = _mm(qk[bi] * d, xv[bi]) + qc[bi] * inter
            hh = num[:, :LANES] / jnp.maximum(jnp.abs(num[:, LANES:]), jnp.exp(-m_t))
            for h in range(nh):
                h_s[rs, h * LANES:(h + 1) * LANES] = hh[h * L:(h + 1) * L, :]
            m_new = perhead_last(m_t)
            w_last = jnp.exp(perhead_last(bcol) - bcol + licol - m_new)
            c_ref[bi] = perhead_last(inter) * c_ref[bi] + _mm_tn(kf[bi] * w_last, xv[bi])
            m_ref[bi] = jnp.broadcast_to(m_new, m_ref.shape[1:])
        return carry

    lax.fori_loop(0, tt // L, body, 0)

    for h in range(nh):
        cs = slice(h * LANES, (h + 1) * LANES)
        hh = h_s[:, cs]
        hh = hh * lax.rsqrt(jnp.mean(hh * hh, axis=-1, keepdims=True) + EPS) * gn_ref[:, cs]
        out_ref[:, :, cs] = (hh * og_s[:, cs]).astype(BF16).reshape(nb, tt, LANES)


def _ssd_kernel(h_ref, ng_ref, w_ref, cw_ref, cb_ref, dtb_ref, alog_ref, dsk_ref, gn_ref, sel_ref,
                ex_ref, out_ref, st_ref, prev_ref, *, nb, tt):
    L = SSD_CHUNK
    assert tt == L
    _first_step_zero(st_ref, prev_ref)

    u = _rms(h_ref[...].reshape(nb * tt, D_MODEL), ng_ref[...])
    fd = _mm(u, w_ref[...])
    z = _silu(fd[:, :512])
    xbc = fd[:, 512:1536]
    conv = _silu(_causal_conv(xbc, [prev_ref[b] for b in range(nb)], cw_ref[...], cb_ref[...], tt))
    _save_carry(prev_ref, xbc, nb, tt)
    x = conv[:, :512]
    bm = conv[:, 512:768].astype(BF16)
    cm = conv[:, 768:1024].astype(BF16)
    lane = lax.broadcasted_iota(jnp.int32, (1, LANES), 1)
    valid = lane < 8
    dt = jnp.where(valid, _softplus(fd[:, 1536:1664] + dtb_ref[...]), 0.0)
    dta = dt * jnp.where(valid, -jnp.exp(alog_ref[...]), 0.0)
    ac = _chunk_sums(sel_ref[...], dta, nb, tt, 3)[0]
    ace = _sel_rhs(ac, ex_ref[...], 3)
    xd = x * _sel_rhs(dt, ex_ref[...], 3)

    ri = lax.broadcasted_iota(jnp.int32, (L, L), 0)
    ci = lax.broadcasted_iota(jnp.int32, (L, L), 1)
    tril = ri >= ci
    m0 = lane < 64
    npair = MIX // LANES
    seqs = range(nb)
    seg = lambda a, bi: a[bi * tt:(bi + 1) * tt]
    groups = [(bi, g) for bi in seqs for g in range(2)]
    chains = [(bi, p) for bi in seqs for p in range(npair)]

    cb = {(bi, g): _mm_nt(seg(cm, bi)[:, g * LANES:(g + 1) * LANES], seg(bm, bi)[:, g * LANES:(g + 1) * LANES])
          for bi, g in groups}
    last = {bi: seg(ace, bi)[L - 1:L, :] for bi in seqs}
    ys = {}
    for bi, p in chains:
        cs = slice(p * LANES, (p + 1) * LANES)
        gs = slice((p // 2) * LANES, (p // 2 + 1) * LANES)
        ys[bi, p] = _mm(seg(cm, bi)[:, gs], st_ref[bi, p]) * jnp.exp(seg(ace, bi)[:, cs])
    for bi, p in chains:
        cs = slice(p * LANES, (p + 1) * LANES)
        gs = slice((p // 2) * LANES, (p // 2 + 1) * LANES)
        xdec = seg(xd, bi)[:, cs] * jnp.exp(last[bi][:, cs] - seg(ace, bi)[:, cs])
        st_ref[bi, p] = st_ref[bi, p] * jnp.exp(last[bi][:, cs]) + _mm_tn(seg(bm, bi)[:, gs], xdec)
    for bi in seqs:
        acc = seg(ac, bi)
        act = acc.T
        for p in range(npair):
            cs = slice(p * LANES, (p + 1) * LANES)
            xp = seg(xd, bi)[:, cs]
            y = ys[bi, p]
            for hh, xm in ((2 * p, jnp.where(m0, xp, 0.0)), (2 * p + 1, jnp.where(m0, 0.0, xp))):
                sg = jnp.exp(jnp.where(tril, acc[:, hh:hh + 1] - act[hh:hh + 1, :], -jnp.inf))
                y = y + _mm(cb[bi, p // 2] * sg, xm)
            ys[bi, p] = y
    y = jnp.concatenate([jnp.concatenate([ys[bi, p] for p in range(npair)], axis=1) for bi in seqs], axis=0)
    y = (y + x * dsk_ref[...]) * z
    out_ref[...] = _rms(y, gn_ref[...]).astype(BF16).reshape(nb, tt, MIX)


def _merge_kernel(h_ref, ng_ref, wg_ref, b0_ref, b1_ref, b2_ref, b3_ref, p_ref, wo_ref, out_ref, u_s, acc_s):
    j = pl.program_id(1)

    @pl.when(j == 0)
    def _init():
        u_s[...] = _rms(h_ref[...], ng_ref[...]).astype(BF16)
        acc_s[...] = jnp.zeros_like(acc_s)

    gate = jax.nn.sigmoid(jnp.dot(u_s[...], wg_ref[...], preferred_element_type=F32))
    for i, br in enumerate((b0_ref, b1_ref, b2_ref, b3_ref)):
        @pl.when(j == i)
        def _acc(br=br):
            acc_s[...] += gate * jnp.dot(br[...], p_ref[...], preferred_element_type=F32)

    @pl.when(j == N_BRANCH - 1)
    def _out():
        out_ref[...] = h_ref[...] + _mm(acc_s[...], wo_ref[...])


def _ffn_kernel(h_ref, ng_ref, upg_ref, upv_ref, cwg_ref, cwv_ref, cbg_ref, cbv_ref, dn_ref, out_ref,
                u_s, acc_s, cg_s, cv_s, *, tm, tiles_per_seq):
    i, j = pl.program_id(0), pl.program_id(1)

    @pl.when(j == 0)
    def _init():
        u_s[...] = _rms(h_ref[...], ng_ref[...]).astype(BF16)
        acc_s[...] = jnp.zeros_like(acc_s)

    seq_start = i % tiles_per_seq == 0

    def half(up_ref, cw_ref, cb_ref, carry_ref):
        z = jnp.dot(u_s[...], up_ref[...], preferred_element_type=F32)
        carry = jnp.where(seq_start, 0.0, carry_ref[j])
        carry_ref[j] = z[tm - CARRY_ROWS:, :]
        return _causal_conv(z, [carry], cw_ref[...], cb_ref[...], tm)

    gate = half(upg_ref, cwg_ref, cbg_ref, cg_s)
    val = half(upv_ref, cwv_ref, cbv_ref, cv_s)
    acc_s[...] += _mm(_silu(gate) * val, dn_ref[...])

    @pl.when(j == pl.num_programs(1) - 1)
    def _out():
        out_ref[...] = h_ref[...] + acc_s[...]


def _norm_kernel(h_ref, g_ref, out_ref):
    out_ref[...] = _rms(h_ref[...], g_ref[...])


def _layer_spec(arr, l):
    tail = arr.shape[1:]
    return pl.BlockSpec((None,) + tail, lambda *_, l=l, n=len(tail): (l,) + (0,) * n)


def _full_spec(arr):
    return pl.BlockSpec(arr.shape, lambda *_, n=arr.ndim: (0,) * n)


def _mixer_call(kern, name, h, l, layer_params, consts, scratch, tt):
    bsz, t, d = h.shape
    in_specs = [pl.BlockSpec((bsz, tt, d), lambda i: (0, i, 0))]
    in_specs += [_layer_spec(a, l) for a in layer_params]
    in_specs += [_full_spec(a) for a in consts]
    return pl.pallas_call(
        functools.partial(kern, nb=bsz, tt=tt),
        grid=(t // tt,),
        in_specs=in_specs,
        out_specs=pl.BlockSpec((bsz, tt, MIX), lambda i: (0, i, 0)),
        out_shape=jax.ShapeDtypeStruct((bsz, t, MIX), BF16),
        scratch_shapes=scratch,
        compiler_params=pltpu.CompilerParams(dimension_semantics=("arbitrary",),
                                             vmem_limit_bytes=VMEM_LIMIT),
        name=name,
    )(h, *layer_params, *consts)


def _vm(shape, dtype=F32):
    return pltpu.VMEM(shape, dtype)


def _block_diag_ones(n, blk, lower=False):
    i = np.arange(n)
    m = (i[:, None] // blk) == (i[None, :] // blk)
    if lower:
        m = m & (i[:, None] >= i[None, :])
    return m


def _chunk_selector(tt, chunk):
    return jnp.asarray(np.concatenate([_block_diag_ones(tt, chunk, lower=True), _block_diag_ones(tt, chunk)]), BF16)


def _pad_lanes(a, width=LANES):
    return jnp.pad(a, [(0, 0)] * (a.ndim - 1) + [(0, width - a.shape[-1])])


def _row(a):
    return a[:, None, :]


def kernel(x, mix_norm_g, w_in, rwkv_mu, rwkv_w0, rwkv_w_up, rwkv_a0, rwkv_a_up, rwkv_g_up, rwkv_k_k, rwkv_k_a, rwkv_r_k, rwkv_gn_g, rwkv_gn_b, gla_f_up, gla_f_bias, gla_norm_g, mlstm_conv_w, mlstm_conv_b, mlstm_i_bias, mlstm_f_bias, mlstm_norm_g, ssd_conv_w, ssd_conv_b, ssd_dt_bias, ssd_a_log, ssd_d, ssd_norm_g, branch_proj, w_out, ffn_norm_g, ffn_up, ffn_conv_w, ffn_conv_b, ffn_down, final_norm_g):
    bsz, t, d = x.shape
    depth = w_in.shape[0]
    tt = MIXER_TILE
    rt = bsz * tt
    assert d == D_MODEL and t % tt == 0

    def cols(lo, hi):
        return w_in[:, :, lo:hi]

    zpad = lambda n: jnp.zeros((depth, d, n), F32)
    w_rwkv = cols(0, RWKV_IN).astype(BF16)
    o = OFF_GLA
    w_gla = jnp.concatenate([cols(o, o + 1024), cols(o + 1040, o + 1552), cols(o + 1024, o + 1040),
                             zpad(LANES - 16)], axis=-1).astype(BF16)
    o = OFF_MLSTM
    w_mlstm = jnp.concatenate([cols(o, o + 1024), cols(o + 1032, o + 1544), cols(o + 1024, o + 1032),
                               zpad(LANES - 8)], axis=-1).astype(BF16)
    o = OFF_SSD
    w_ssd = jnp.concatenate([cols(o, o + 1544), zpad(LANES - 8)], axis=-1).astype(BF16)
    w_gate = cols(OFF_GATE, OFF_GATE + N_BRANCH * d).reshape(depth, d, N_BRANCH, d)
    w_gate = jnp.transpose(w_gate, (0, 2, 1, 3)).astype(BF16)

    ng = _row(mix_norm_g)
    zl = jnp.zeros_like(rwkv_w_up)
    rwkv_lora = jnp.concatenate([jnp.concatenate([rwkv_w_up, zl], axis=-1),
                                 jnp.concatenate([zl, rwkv_a_up], axis=-1)], axis=1).astype(BF16)
    rwkv_params = [ng, w_rwkv, _row(rwkv_mu), _row(rwkv_w0), _row(rwkv_a0), rwkv_lora,
                   rwkv_g_up.astype(BF16), _row(rwkv_k_k), _row(rwkv_k_a),
                   _row(rwkv_r_k.reshape(depth, MIX)), _row(rwkv_gn_g), _row(rwkv_gn_b)]
    rwkv_consts = [jnp.asarray(_block_diag_ones(MIX, 64), BF16), _chunk_selector(tt, RWKV_CHUNK)]
    rwkv_scratch = ([_vm((bsz, MIX // LANES, LANES, LANES)), _vm((bsz, CARRY_ROWS, RWKV_IN))]
                    + [_vm((rt, MIX), BF16)] * 7 + [_vm((rt, MIX))] * 4)

    gla_fup = jnp.pad(gla_f_up, ((0, 0), (0, LANES - gla_f_up.shape[1]), (0, 0))).astype(BF16)
    gla_params = [ng, w_gla, gla_fup, _row(gla_f_bias), _row(gla_norm_g)]
    gla_consts = [_chunk_selector(tt, GLA_CHUNK)]
    gla_scratch = [_vm((bsz, MIX, 256)), _vm((rt, 256), BF16), _vm((rt, 256), BF16), _vm((rt, 256), BF16),
                   _vm((rt, MIX), BF16), _vm((rt, 256)), _vm((rt, MIX)), _vm((rt, MIX))]

    zh = jnp.zeros_like(mlstm_i_bias)
    mlstm_params = [ng, w_mlstm, mlstm_conv_w, _row(mlstm_conv_b),
                    _row(_pad_lanes(jnp.concatenate([mlstm_i_bias, zh], axis=-1))),
                    _row(_pad_lanes(jnp.concatenate([zh, mlstm_f_bias], axis=-1))),
                    _row(mlstm_norm_g)]
    mlstm_consts = [_chunk_selector(tt, MLSTM_CHUNK)]
    mlstm_scratch = [_vm((bsz, 256, 256)), _vm((bsz, 256, LANES)), _vm((bsz, CARRY_ROWS, 512)),
                     _vm((rt, 256), BF16), _vm((rt, 256)), _vm((rt, MIX), BF16), _vm((rt, LANES)),
                     _vm((rt, LANES)), _vm((rt, MIX)), _vm((rt, MIX))]

    ssd_params = [ng, w_ssd, ssd_conv_w, _row(ssd_conv_b), _row(_pad_lanes(ssd_dt_bias)),
                  _row(_pad_lanes(ssd_a_log)), _row(jnp.repeat(ssd_d, 64, axis=-1)), _row(ssd_norm_g)]
    hid = np.arange(MIX) // 64
    expand = jnp.asarray(np.arange(LANES)[:, None] == hid[None, :], BF16)
    ssd_consts = [_chunk_selector(tt, SSD_CHUNK), expand]
    ssd_scratch = [_vm((bsz, MIX // LANES, LANES, LANES)), _vm((bsz, CARRY_ROWS, 1024))]

    bproj = branch_proj.astype(BF16)
    wout = w_out.astype(BF16)
    fng = _row(ffn_norm_g)
    fup = ffn_up.astype(BF16)
    fdn = ffn_down.astype(BF16)
    fcb = _row(ffn_conv_b)

    m = bsz * t
    tm = 512
    tf = D_FF // 2
    nf = D_FF // tf
    assert t % tm == 0
    cparams = pltpu.CompilerParams(dimension_semantics=("parallel", "arbitrary"), vmem_limit_bytes=VMEM_LIMIT)

    def merge(h2, l, branches):
        row_spec = pl.BlockSpec((tm, d), lambda i, j: (i, 0))
        br_spec = pl.BlockSpec((tm, MIX), lambda i, j: (i, 0))
        return pl.pallas_call(
            _merge_kernel,
            grid=(m // tm, N_BRANCH),
            in_specs=[row_spec, _layer_spec(ng, l),
                      pl.BlockSpec((None, None, d, d), lambda i, j: (l, j, 0, 0)),
                      br_spec, br_spec, br_spec, br_spec,
                      pl.BlockSpec((None, None, MIX, d), lambda i, j: (l, j, 0, 0)),
                      _layer_spec(wout, l)],
            out_specs=row_spec,
            out_shape=jax.ShapeDtypeStruct((m, d), F32),
            scratch_shapes=[_vm((tm, d), BF16), _vm((tm, d))],
            compiler_params=cparams,
            name="merge",
        )(h2, ng, w_gate, *branches, bproj, wout)

    def ffn(h2, l):
        row_spec = pl.BlockSpec((tm, d), lambda i, j: (i, 0))
        return pl.pallas_call(
            functools.partial(_ffn_kernel, tm=tm, tiles_per_seq=t // tm),
            grid=(m // tm, nf),
            in_specs=[row_spec, _layer_spec(fng, l),
                      pl.BlockSpec((None, d, tf), lambda i, j: (l, 0, j)),
                      pl.BlockSpec((None, d, tf), lambda i, j: (l, 0, nf + j)),
                      pl.BlockSpec((None, 3, tf), lambda i, j: (l, 0, j)),
                      pl.BlockSpec((None, 3, tf), lambda i, j: (l, 0, nf + j)),
                      pl.BlockSpec((None, 1, tf), lambda i, j: (l, 0, j)),
                      pl.BlockSpec((None, 1, tf), lambda i, j: (l, 0, nf + j)),
                      pl.BlockSpec((None, tf, d), lambda i, j: (l, j, 0))],
            out_specs=row_spec,
            out_shape=jax.ShapeDtypeStruct((m, d), F32),
            scratch_shapes=[_vm((tm, d), BF16), _vm((tm, d)), _vm((nf, CARRY_ROWS, tf)),
                            _vm((nf, CARRY_ROWS, tf))],
            compiler_params=cparams,
            name="ffn",
        )(h2, fng, fup, fup, ffn_conv_w, ffn_conv_w, fcb, fcb, fdn)

    h = x
    for l in range(depth):
        branches = [
            _mixer_call(_rwkv_kernel, "rwkv", h, l, rwkv_params, rwkv_consts, rwkv_scratch, tt),
            _mixer_call(_gla_kernel, "gla", h, l, gla_params, gla_consts, gla_scratch, tt),
            _mixer_call(_mlstm_kernel, "mlstm", h, l, mlstm_params, mlstm_consts, mlstm_scratch, tt),
            _mixer_call(_ssd_kernel, "ssd", h, l, ssd_params, ssd_consts, ssd_scratch, tt),
        ]
        h2 = merge(h.reshape(m, d), l, [b.reshape(m, MIX) for b in branches])
        h = ffn(h2, l).reshape(bsz, t, d)

    out = pl.pallas_call(
        _norm_kernel,
        grid=(m // tm,),
        in_specs=[pl.BlockSpec((tm, d), lambda i: (i, 0)), pl.BlockSpec((1, d), lambda i: (0, 0))],
        out_specs=pl.BlockSpec((tm, d), lambda i: (i, 0)),
        out_shape=jax.ShapeDtypeStruct((m, d), x.dtype),
        name="final_norm",
    )(h.reshape(m, d), final_norm_g[None, :])
    return out.reshape(bsz, t, d)
```

```python
import functools

import numpy as np
import jax
import jax.numpy as jnp
from jax import lax
from jax.experimental import pallas as pl
from jax.experimental.pallas import tpu as pltpu

F32, BF16 = jnp.float32, jnp.bfloat16

D_MODEL = 1024
MIX = 512
EPS = 1e-6
N_BRANCH = 4
RWKV_DECAY_SCALE = 0.6065306597
RWKV_GN_EPS = 64e-5
RWKV_CHUNK = 64
GLA_CHUNK = 16
GLA_GROUP = 4
MLSTM_CHUNK = 64
SSD_CHUNK = 128
D_FF = 2816

RWKV_IN, GLA_IN, MLSTM_IN, SSD_IN = 1792, 1552, 1544, 1544
OFF_GLA = RWKV_IN
OFF_MLSTM = OFF_GLA + GLA_IN
OFF_SSD = OFF_MLSTM + MLSTM_IN
OFF_GATE = OFF_SSD + SSD_IN

LANES = 128
CARRY_ROWS = 8
MIXER_TILE = 128
VMEM_LIMIT = 48 * 1024 * 1024


def _mm(a, b):
    return jnp.dot(a.astype(BF16), b.astype(BF16), preferred_element_type=F32)


def _mm_nt(a, b):
    return lax.dot_general(a.astype(BF16), b.astype(BF16), (((1,), (1,)), ((), ())),
                           preferred_element_type=F32)


def _mm_tn(a, b):
    return lax.dot_general(a.astype(BF16), b.astype(BF16), (((0,), (0,)), ((), ())),
                           preferred_element_type=F32)


def _bf16_terms(x, n):
    terms, rest = [], x
    for _ in range(n):
        t = rest.astype(BF16)
        terms.append(t)
        rest = rest - t.astype(F32)
    return terms


def _sel_lhs(sel, x, n=3):
    out = None
    for t in _bf16_terms(x, n):
        p = jnp.dot(sel, t, preferred_element_type=F32)
        out = p if out is None else out + p
    return out


def _sel_rhs(x, sel, n=2):
    out = None
    for t in _bf16_terms(x, n):
        p = jnp.dot(t, sel, preferred_element_type=F32)
        out = p if out is None else out + p
    return out


def _chunk_sums(sel, x, nb, tt, n=2):
    parts = [_sel_lhs(sel, x[b * tt:(b + 1) * tt], n) for b in range(nb)]
    return (jnp.concatenate([p[:tt] for p in parts], axis=0),
            jnp.concatenate([p[tt:] for p in parts], axis=0))


def _rms(x, g):
    return x * lax.rsqrt(jnp.mean(x * x, axis=-1, keepdims=True) + EPS) * g


def _silu(x):
    return x * jax.nn.sigmoid(x)


def _softplus(x):
    return jnp.maximum(x, 0.0) + jnp.log1p(jnp.exp(-jnp.abs(x)))


def _log_sigmoid(x):
    return -_softplus(-x)


def _shift_rows(x, carry, s):
    y = pltpu.roll(x, s, 0)
    rows = lax.broadcasted_iota(jnp.int32, (CARRY_ROWS, 1), 0)
    head = y[:CARRY_ROWS]
    for q in range(s):
        src = CARRY_ROWS - s + q
        head = jnp.where(rows == q, carry[src:src + 1, :], head)
    return jnp.concatenate([head, y[CARRY_ROWS:]], axis=0)


def _causal_conv(x, carries, w, b, tt):
    k = w.shape[0]
    y = w[k - 1:k, :] * x + b
    for s in range(1, k):
        sh = jnp.concatenate([_shift_rows(x[i * tt:(i + 1) * tt], c, s) for i, c in enumerate(carries)],
                             axis=0)
        y = y + w[k - 1 - s:k - s, :] * sh
    return y


def _save_carry(carry_ref, x, nb, tt):
    for b in range(nb):
        carry_ref[b] = x[(b + 1) * tt - CARRY_ROWS:(b + 1) * tt, :]


def _first_step_zero(*refs):
    @pl.when(pl.program_id(0) == 0)
    def _init():
        for r in refs:
            r[...] = jnp.zeros_like(r)


def _rwkv_kernel(h_ref, ng_ref, w_ref, mu_ref, w0_ref, a0_ref, lora_ref, gup_ref, kk_ref, ka_ref,
                 rk_ref, gng_ref, gnb_ref, bd_ref, sel_ref, out_ref,
                 st_ref, prev_ref, ab_s, rb_s, bt_s, kt_s, v_s, bg_s, kg_s, gl_s, y_s, bon_s, g_s,
                 *, nb, tt):
    L = RWKV_CHUNK
    npair = MIX // LANES
    _first_step_zero(st_ref, prev_ref)

    u = _rms(h_ref[...].reshape(nb * tt, D_MODEL), ng_ref[...])
    fa = _mm(u, w_ref[...])
    prev = jnp.concatenate([_shift_rows(fa[b * tt:(b + 1) * tt], prev_ref[b], 1) for b in range(nb)], axis=0)
    _save_carry(prev_ref, fa, nb, tt)
    fa = fa + mu_ref[...] * (prev - fa)
    r, k, v = fa[:, :512], fa[:, 512:1024], fa[:, 1024:1536]
    lo = fa[:, 1536:1664]
    lane = lax.broadcasted_iota(jnp.int32, (1, LANES), 1)
    pre = _mm(jnp.where(lane < 64, jnp.tanh(lo), lo), lora_ref[...])
    lw = -RWKV_DECAY_SCALE * jax.nn.sigmoid(w0_ref[...] + pre[:, :512])
    a = jax.nn.sigmoid(a0_ref[...] + pre[:, 512:])
    g_s[...] = _mm(jax.nn.sigmoid(fa[:, 1664:1792]), gup_ref[...])
    bd = bd_ref[...]

    def head_sum(x):
        half = bd.shape[0]
        return jnp.concatenate([_sel_rhs(x[:, i:i + half], bd, 1) for i in range(0, MIX, half)], axis=1)

    kk = k * kk_ref[...]
    kk = kk * lax.rsqrt(jnp.maximum(head_sum(kk * kk), 1e-24))
    k = k * (1.0 + (a - 1.0) * ka_ref[...])
    bon_s[...] = head_sum(r * k * rk_ref[...]) * v
    b, bl = _chunk_sums(sel_ref[...], lw, nb, tt)
    eb, enb, et = jnp.exp(b), jnp.exp(-b), jnp.exp(bl - b)
    beta = kk * a
    ab_s[...] = (-kk * jnp.exp(b - lw)).astype(BF16)
    rb_s[...] = (r * eb).astype(BF16)
    bt_s[...] = (beta * enb).astype(BF16)
    kt_s[...] = (k * enb).astype(BF16)
    v_s[...] = v.astype(BF16)
    bg_s[...] = (beta * et).astype(BF16)
    kg_s[...] = (k * et).astype(BF16)
    gl_s[...] = jnp.exp(bl)

    ri = lax.broadcasted_iota(jnp.int32, (2 * L, 2 * L), 0)
    ci = lax.broadcasted_iota(jnp.int32, (2 * L, 2 * L), 1)
    same = (ri // L) == (ci // L)
    strict = jnp.logical_and(same, (ri % L) > (ci % L))
    incl = jnp.logical_and(same, (ri % L) >= (ci % L))
    eye = jnp.where(ri == ci, 1.0, 0.0).astype(F32)
    m0 = lane < 64
    chains = [(bi, p) for bi in range(nb) for p in range(npair)]

    def body(c, carry):
        r0 = pl.multiple_of(c * L, L)

        def ld(ref, bi, p):
            x = ref[pl.ds(bi * tt + r0, L), p * LANES:(p + 1) * LANES]
            zero = jnp.zeros_like(x)
            return jnp.concatenate([jnp.where(m0, x, zero), jnp.where(m0, zero, x)], axis=0)

        def ld2(ref_a, ref_b, bi, p):
            return jnp.concatenate([ld(ref_a, bi, p), ld(ref_b, bi, p)], axis=0)

        gm = [_mm_nt(ld2(ab_s, rb_s, bi, p), ld2(bt_s, kt_s, bi, p)) for bi, p in chains]
        a_ab = [jnp.where(strict, x[:2 * L, :2 * L], 0.0) for x in gm]
        a_ak = [jnp.where(strict, x[:2 * L, 2 * L:], 0.0).astype(BF16) for x in gm]
        a_rb = [jnp.where(incl, x[2 * L:, :2 * L], 0.0).astype(BF16) for x in gm]
        a_rk = [jnp.where(incl, x[2 * L:, 2 * L:], 0.0).astype(BF16) for x in gm]
        tinv = [eye + x for x in a_ab]
        pw = [x.astype(BF16) for x in a_ab]
        for _ in range(int(np.log2(L)) - 1):
            pw = [_mm(x, x).astype(BF16) for x in pw]
            tinv = [t + _mm(t, x) for t, x in zip(tinv, pw)]
        st = [st_ref[bi, p] for bi, p in chains]
        stb = [x.astype(BF16) for x in st]
        w1 = [_mm_nt(ld(ab_s, bi, p), s) + _mm(ak, ld(v_s, bi, p))
              for (bi, p), s, ak in zip(chains, stb, a_ak)]
        uu = [_mm(t, x).astype(BF16) for t, x in zip(tinv, w1)]
        yy = [_mm_nt(ld(rb_s, bi, p), s) + _mm(rb, x) + _mm(rk, ld(v_s, bi, p))
              for (bi, p), s, rb, rk, x in zip(chains, stb, a_rb, a_rk, uu)]
        for (bi, p), y in zip(chains, yy):
            y_s[pl.ds(bi * tt + r0, L), p * LANES:(p + 1) * LANES] = y[:L] + y[L:]
        upd = [_mm_tn(jnp.concatenate([x, ld(v_s, bi, p)], axis=0), ld2(bg_s, kg_s, bi, p))
               for (bi, p), x in zip(chains, uu)]
        for (bi, p), s, d in zip(chains, st, upd):
            st_ref[bi, p] = s * gl_s[pl.ds(bi * tt + r0, 1), p * LANES:(p + 1) * LANES] + d
        return carry

    lax.fori_loop(0, tt // L, body, 0)

    y = y_s[...]
    yc = y - head_sum(y) * (1.0 / 64)
    var = head_sum(yc * yc) * (1.0 / 64)
    yn = yc * lax.rsqrt(var + RWKV_GN_EPS) * gng_ref[...] + gnb_ref[...]
    out_ref[...] = ((yn + bon_s[...]) * g_s[...]).astype(BF16).reshape(nb, tt, MIX)


def _gla_kernel(h_ref, ng_ref, w_ref, fup_ref, fb_ref, gn_ref, sel_ref, out_ref,
                st_ref, qd_s, kd_s, kg_s, v_s, bl_s, o_s, og_s, *, nb, tt):
    L = GLA_CHUNK
    nh = 4
    nc = GLA_GROUP
    _first_step_zero(st_ref)

    u = _rms(h_ref[...].reshape(nb * tt, D_MODEL), ng_ref[...])
    fb = _mm(u, w_ref[...])
    q, k = fb[:, :256] * 0.125, fb[:, 256:512]
    v_s[...] = fb[:, 512:1024].astype(BF16)
    og_s[...] = _silu(fb[:, 1024:1536])
    la = _log_sigmoid(_mm(fb[:, 1536:1664], fup_ref[...]) + fb_ref[...]) * (1.0 / 16.0)
    b, bl = _chunk_sums(sel_ref[...], la, nb, tt)
    qd_s[...] = q * jnp.exp(b)
    kd_s[...] = (k * jnp.exp(-b)).astype(BF16)
    kg_s[...] = k * jnp.exp(bl - b)
    bl_s[...] = bl

    hq = lax.broadcasted_iota(jnp.int32, (1, 256), 1) // 64
    hv = lax.broadcasted_iota(jnp.int32, (1, 512), 1) // 128
    def key_mask(c):
        ri = lax.broadcasted_iota(jnp.int32, (nh * L, nh * L * (c + 1)), 0)
        ci = lax.broadcasted_iota(jnp.int32, (nh * L, nh * L * (c + 1)), 1)
        own = jnp.logical_and((ri // L) == (ci // L), (ri % L) >= (ci % L))
        return jnp.logical_or(ci >= nh * L, own)

    masks = [key_mask(c) for c in range(nc)]
    seqs = range(nb)

    def stack(x, hid):
        x = x.astype(BF16)
        zero = jnp.zeros_like(x)
        return jnp.concatenate([jnp.where(hid == h, x, zero) for h in range(nh)], axis=0)

    def body(g, carry):
        r0 = pl.multiple_of(g * (nc * L), nc * L)
        keys, vals, qg, kgs, gtot = [], [], [], [], []
        for bi in seqs:
            rows = [pl.ds(bi * tt + r0 + L * c, L) for c in range(nc)]
            tot = [bl_s[pl.ds(bi * tt + r0 + L * c, 1), :] for c in range(nc)]

            def span(lo, hi):
                acc = None
                for c in range(lo, hi):
                    acc = tot[c] if acc is None else acc + tot[c]
                return 1.0 if acc is None else jnp.exp(acc)

            xq = [qd_s[r, :] for r in rows]
            xkg = [kg_s[r, :] for r in rows]
            xv = [stack(v_s[r, :], hv) for r in rows]
            keys.append([jnp.concatenate([stack(kd_s[rows[c], :], hq)]
                                         + [stack(xkg[j] * span(j + 1, c), hq) for j in range(c - 1, -1, -1)], axis=0)
                         for c in range(nc)])
            vals.append([jnp.concatenate([xv[c]] + [xv[j] for j in range(c - 1, -1, -1)], axis=0) for c in range(nc)])
            qg.append([stack(xq[c], hq) for c in range(nc)]
                      + [jnp.concatenate([stack(xq[c] * span(0, c), hq) for c in range(nc)], axis=0)])
            kgs.append((jnp.concatenate(xv, axis=0),
                        jnp.concatenate([stack(xkg[c] * span(c + 1, nc), hq) for c in range(nc)], axis=0)))
            gtot.append(span(0, nc))
        att = [[jnp.where(masks[c], _mm_nt(qg[bi][c], keys[bi][c]), 0.0) for c in range(nc)] for bi in seqs]
        upd = [_mm_tn(*kgs[bi]) for bi in seqs]
        st = [st_ref[bi] for bi in seqs]
        oi = [_mm_nt(qg[bi][nc], st[bi]) for bi in seqs]
        for bi in seqs:
            for c in range(nc):
                oo = _mm(att[bi][c], vals[bi][c]) + oi[bi][c * nh * L:(c + 1) * nh * L]
                acc = oo[:L]
                for h in range(1, nh):
                    acc = acc + oo[h * L:(h + 1) * L]
                o_s[pl.ds(bi * tt + r0 + L * c, L), :] = acc
            st_ref[bi] = st[bi] * gtot[bi] + upd[bi]
        return carry

    lax.fori_loop(0, tt // (nc * L), body, 0)

    for h in range(nh):
        cs = slice(h * LANES, (h + 1) * LANES)
        oh = o_s[:, cs]
        oh = oh * lax.rsqrt(jnp.mean(oh * oh, axis=-1, keepdims=True) + EPS) * gn_ref[...]
        out_ref[:, :, cs] = (oh * og_s[:, cs]).astype(BF16).reshape(nb, tt, LANES)


def _mlstm_kernel(h_ref, ng_ref, w_ref, cw_ref, cb_ref, ib_ref, fbias_ref, gn_ref, sel_ref, eli_ref, eb_ref,
                  out_ref, c_ref, m_ref, prev_ref, q_s, k_s, v_s, bf_s, lf_s, lib_s, h_s, og_s, *, nb, tt):
    L = MLSTM_CHUNK
    nh = 4
    _first_step_zero(c_ref, m_ref, prev_ref)

    u = _rms(h_ref[...].reshape(nb * tt, D_MODEL), ng_ref[...])
    fc = _mm(u, w_ref[...])
    qk = fc[:, :512]
    conv = _causal_conv(qk, [prev_ref[b] for b in range(nb)], cw_ref[...], cb_ref[...], tt)
    _save_carry(prev_ref, qk, nb, tt)
    qk = _silu(conv)
    q_s[...] = qk[:, :256].astype(BF16)
    k_s[...] = qk[:, 256:] * 0.125
    v_s[...] = fc[:, 512:1024].astype(BF16)
    og_s[...] = jax.nn.sigmoid(fc[:, 1024:1536])
    misc = fc[:, 1536:1664]
    lane = lax.broadcasted_iota(jnp.int32, (1, LANES), 1)
    li = misc + ib_ref[...]
    b = _chunk_sums(sel_ref[...], _log_sigmoid(misc + fbias_ref[...]), nb, tt)[0]
    lf_s[...] = _sel_rhs(li, eli_ref[...], 3)
    bf_s[...] = _sel_rhs(b, eb_ref[...], 3)
    lib_s[...] = jnp.where(lane < nh, li, jnp.where(lane < 2 * nh, -b, 0.0))

    hq = lax.broadcasted_iota(jnp.int32, (1, 256), 1) // 64
    ri = lax.broadcasted_iota(jnp.int32, (nh * L, nh * L), 0)
    ci = lax.broadcasted_iota(jnp.int32, (nh * L, nh * L), 1)
    dmask = jnp.logical_and((ri // L) == (ci // L), (ri % L) >= (ci % L))
    ones = jnp.ones((L, LANES), BF16)
    ones_rows = jnp.ones((nh * L, LANES), BF16)
    seqs = range(nb)

    def stackq(x):
        zero = jnp.zeros_like(x)
        return jnp.concatenate([jnp.where(hq == h, x, zero) for h in range(nh)], axis=0)

    def headcol(ref, rs):
        x = ref[rs, :]
        return jnp.concatenate([x[:, h * LANES:(h + 1) * LANES] for h in range(nh)], axis=0)

    def perhead_last(col):
        return jnp.concatenate(
            [jnp.broadcast_to(col[(h + 1) * L - 1:(h + 1) * L, :], (L, LANES)) for h in range(nh)], axis=0)

    def wide(col):
        return jnp.concatenate([col, col], axis=1)

    def body(c, carry):
        r0 = pl.multiple_of(c * L, L)
        rows = [pl.ds(bi * tt + r0, L) for bi in seqs]
        xq = [stackq(q_s[rs, :]) for rs in rows]
        kf = [stackq(k_s[rs, :]) for rs in rows]
        xv = []
        for rs in rows:
            vc = v_s[rs, :]
            xv.append(jnp.concatenate(
                [jnp.concatenate([vc[:, h * LANES:(h + 1) * LANES], ones], axis=1) for h in range(nh)], axis=0))
        qk = [_mm_nt(xq[bi], kf[bi]) for bi in seqs]
        qc = [_mm(xq[bi], c_ref[bi]) for bi in seqs]
        drow = []
        for rs in rows:
            x = lib_s[rs, :]
            zx = jnp.concatenate(
                [jnp.where(jnp.logical_or(lane == h, lane == nh + h), x, 0.0) for h in range(nh)], axis=0)
            acc = None
            for t in _bf16_terms(zx, 3):
                p = lax.dot_general(ones_rows, t, (((1,), (1,)), ((), ())), preferred_element_type=F32)
                acc = p if acc is None else acc + p
            drow.append(acc)
        for bi in seqs:
            rs = rows[bi]
            bcol, licol = headcol(bf_s, rs), headcol(lf_s, rs)
            mcol = m_ref[bi]
            log_d = jnp.where(dmask, wide(bcol) + drow[bi], -jnp.inf)
            mx = jnp.broadcast_to(jnp.max(log_d, axis=-1, keepdims=True), (nh * L, LANES))
            m_t = jnp.maximum(bcol + mcol, mx)
            d = jnp.exp(log_d - wide(m_t))
            inter = jnp.exp(bcol + mcol - m_t)
            num = _mm(qk[bi] * d, xv[bi]) + qc[bi] * wide(inter)
            hh = num[:, :LANES] / jnp.maximum(jnp.abs(num[:, LANES:]), jnp.exp(-m_t))
            for h in range(nh):
                h_s[rs, h * LANES:(h + 1) * LANES] = hh[h * L:(h + 1) * L, :]
            m_new = perhead_last(m_t)
            w_last = jnp.exp(perhead_last(bcol) - bcol + licol - m_new)
            c_ref[bi] = wide(perhead_last(inter)) * c_ref[bi] + _mm_tn(kf[bi] * wide(w_last), xv[bi])
            m_ref[bi] = m_new
        return carry

    lax.fori_loop(0, tt // L, body, 0)

    for h in range(nh):
        cs = slice(h * LANES, (h + 1) * LANES)
        hh = h_s[:, cs]
        hh = hh * lax.rsqrt(jnp.mean(hh * hh, axis=-1, keepdims=True) + EPS) * gn_ref[:, cs]
        out_ref[:, :, cs] = (hh * og_s[:, cs]).astype(BF16).reshape(nb, tt, LANES)


def _ssd_kernel(h_ref, ng_ref, w_ref, cw_ref, cb_ref, dtb_ref, alog_ref, dsk_ref, gn_ref, sel_ref,
                ex_ref, out_ref, st_ref, prev_ref, *, nb, tt):
    L = SSD_CHUNK
    assert tt == L
    _first_step_zero(st_ref, prev_ref)

    u = _rms(h_ref[...].reshape(nb * tt, D_MODEL), ng_ref[...])
    fd = _mm(u, w_ref[...])
    z = _silu(fd[:, :512])
    xbc = fd[:, 512:1536]
    conv = _silu(_causal_conv(xbc, [prev_ref[b] for b in range(nb)], cw_ref[...], cb_ref[...], tt))
    _save_carry(prev_ref, xbc, nb, tt)
    x = conv[:, :512]
    bm = conv[:, 512:768].astype(BF16)
    cm = conv[:, 768:1024].astype(BF16)
    lane = lax.broadcasted_iota(jnp.int32, (1, LANES), 1)
    valid = lane < 8
    dt = jnp.where(valid, _softplus(fd[:, 1536:1664] + dtb_ref[...]), 0.0)
    dta = dt * jnp.where(valid, -jnp.exp(alog_ref[...]), 0.0)
    ac = _chunk_sums(sel_ref[...], dta, nb, tt, 3)[0]
    ace = _sel_rhs(ac, ex_ref[...], 3)
    xd = x * _sel_rhs(dt, ex_ref[...], 3)

    ri = lax.broadcasted_iota(jnp.int32, (L, L), 0)
    ci = lax.broadcasted_iota(jnp.int32, (L, L), 1)
    tril = ri >= ci
    m0 = lane < 64
    npair = MIX // LANES
    seqs = range(nb)
    seg = lambda a, bi: a[bi * tt:(bi + 1) * tt]
    groups = [(bi, g) for bi in seqs for g in range(2)]
    chains = [(bi, p) for bi in seqs for p in range(npair)]

    cb = {(bi, g): _mm_nt(seg(cm, bi)[:, g * LANES:(g + 1) * LANES], seg(bm, bi)[:, g * LANES:(g + 1) * LANES])
          for bi, g in groups}
    last = {bi: seg(ace, bi)[L - 1:L, :] for bi in seqs}
    ys = {}
    for bi, p in chains:
        cs = slice(p * LANES, (p + 1) * LANES)
        gs = slice((p // 2) * LANES, (p // 2 + 1) * LANES)
        ys[bi, p] = _mm(seg(cm, bi)[:, gs], st_ref[bi, p]) * jnp.exp(seg(ace, bi)[:, cs])
    for bi, p in chains:
        cs = slice(p * LANES, (p + 1) * LANES)
        gs = slice((p // 2) * LANES, (p // 2 + 1) * LANES)
        xdec = seg(xd, bi)[:, cs] * jnp.exp(last[bi][:, cs] - seg(ace, bi)[:, cs])
        st_ref[bi, p] = st_ref[bi, p] * jnp.exp(last[bi][:, cs]) + _mm_tn(seg(bm, bi)[:, gs], xdec)
    for bi in seqs:
        acc = seg(ac, bi)
        act = acc.T
        for p in range(npair):
            cs = slice(p * LANES, (p + 1) * LANES)
            xp = seg(xd, bi)[:, cs]
            y = ys[bi, p]
            for hh, xm in ((2 * p, jnp.where(m0, xp, 0.0)), (2 * p + 1, jnp.where(m0, 0.0, xp))):
                sg = jnp.exp(jnp.where(tril, acc[:, hh:hh + 1] - act[hh:hh + 1, :], -jnp.inf))
                y = y + _mm(cb[bi, p // 2] * sg, xm)
            ys[bi, p] = y
    y = jnp.concatenate([jnp.concatenate([ys[bi, p] for p in range(npair)], axis=1) for bi in seqs], axis=0)
    y = (y + x * dsk_ref[...]) * z
    out_ref[...] = _rms(y, gn_ref[...]).astype(BF16).reshape(nb, tt, MIX)


def _merge_kernel(h_ref, ng_ref, wg_ref, b0_ref, b1_ref, b2_ref, b3_ref, p_ref, wo_ref, out_ref):
    x = h_ref[...]
    u = _rms(x, ng_ref[...]).astype(BF16)
    acc = None
    for i, br in enumerate((b0_ref, b1_ref, b2_ref, b3_ref)):
        gate = jax.nn.sigmoid(jnp.dot(u, wg_ref[i], preferred_element_type=F32))
        term = gate * jnp.dot(br[...], p_ref[i], preferred_element_type=F32)
        acc = term if acc is None else acc + term
    out_ref[...] = x + _mm(acc, wo_ref[...])


def _ffn_kernel(h_ref, ng_ref, up_ref, cw_ref, cb_ref, dn_ref, out_ref, a_s, carry_s, *, tm, tiles_per_seq, tf):
    x = h_ref[...]
    u = _rms(x, ng_ref[...]).astype(BF16)
    seq_start = pl.program_id(0) % tiles_per_seq == 0

    def half(off):
        cs = slice(off, off + tf)
        z = jnp.dot(u, up_ref[:, cs], preferred_element_type=F32)
        carry = jnp.where(seq_start, 0.0, carry_s[:, cs])
        carry_s[:, cs] = z[tm - CARRY_ROWS:, :]
        return _causal_conv(z, [carry], cw_ref[:, cs], cb_ref[:, cs], tm)

    for j in range(D_FF // tf):
        gate, val = half(j * tf), half(D_FF + j * tf)
        a_s[:, j * tf:(j + 1) * tf] = (_silu(gate) * val).astype(BF16)
    out_ref[...] = x + jnp.dot(a_s[...], dn_ref[...], preferred_element_type=F32)


def _norm_kernel(h_ref, g_ref, out_ref):
    out_ref[...] = _rms(h_ref[...], g_ref[...])


def _layer_spec(arr, l):
    tail = arr.shape[1:]
    return pl.BlockSpec((None,) + tail, lambda *_, l=l, n=len(tail): (l,) + (0,) * n)


def _full_spec(arr):
    return pl.BlockSpec(arr.shape, lambda *_, n=arr.ndim: (0,) * n)


def _mixer_call(kern, name, h, l, layer_params, consts, scratch, tt):
    bsz, t, d = h.shape
    in_specs = [pl.BlockSpec((bsz, tt, d), lambda i: (0, i, 0))]
    in_specs += [_layer_spec(a, l) for a in layer_params]
    in_specs += [_full_spec(a) for a in consts]
    return pl.pallas_call(
        functools.partial(kern, nb=bsz, tt=tt),
        grid=(t // tt,),
        in_specs=in_specs,
        out_specs=pl.BlockSpec((bsz, tt, MIX), lambda i: (0, i, 0)),
        out_shape=jax.ShapeDtypeStruct((bsz, t, MIX), BF16),
        scratch_shapes=scratch,
        compiler_params=pltpu.CompilerParams(dimension_semantics=("arbitrary",),
                                             vmem_limit_bytes=VMEM_LIMIT),
        name=name,
    )(h, *layer_params, *consts)


def _vm(shape, dtype=F32):
    return pltpu.VMEM(shape, dtype)


def _block_diag_ones(n, blk, lower=False):
    i = np.arange(n)
    m = (i[:, None] // blk) == (i[None, :] // blk)
    if lower:
        m = m & (i[:, None] >= i[None, :])
    return m


def _chunk_selector(tt, chunk):
    return jnp.asarray(np.concatenate([_block_diag_ones(tt, chunk, lower=True), _block_diag_ones(tt, chunk)]), BF16)


def _pad_lanes(a, width=LANES):
    return jnp.pad(a, [(0, 0)] * (a.ndim - 1) + [(0, width - a.shape[-1])])


def _row(a):
    return a[:, None, :]


def kernel(x, mix_norm_g, w_in, rwkv_mu, rwkv_w0, rwkv_w_up, rwkv_a0, rwkv_a_up, rwkv_g_up, rwkv_k_k, rwkv_k_a, rwkv_r_k, rwkv_gn_g, rwkv_gn_b, gla_f_up, gla_f_bias, gla_norm_g, mlstm_conv_w, mlstm_conv_b, mlstm_i_bias, mlstm_f_bias, mlstm_norm_g, ssd_conv_w, ssd_conv_b, ssd_dt_bias, ssd_a_log, ssd_d, ssd_norm_g, branch_proj, w_out, ffn_norm_g, ffn_up, ffn_conv_w, ffn_conv_b, ffn_down, final_norm_g):
    bsz, t, d = x.shape
    depth = w_in.shape[0]
    tt = MIXER_TILE
    rt = bsz * tt
    assert d == D_MODEL and t % tt == 0

    def cols(lo, hi):
        return w_in[:, :, lo:hi]

    zpad = lambda n: jnp.zeros((depth, d, n), F32)
    w_rwkv = cols(0, RWKV_IN).astype(BF16)
    o = OFF_GLA
    w_gla = jnp.concatenate([cols(o, o + 1024), cols(o + 1040, o + 1552), cols(o + 1024, o + 1040),
                             zpad(LANES - 16)], axis=-1).astype(BF16)
    o = OFF_MLSTM
    w_mlstm = jnp.concatenate([cols(o, o + 1024), cols(o + 1032, o + 1544), cols(o + 1024, o + 1032),
                               zpad(LANES - 8)], axis=-1).astype(BF16)
    o = OFF_SSD
    w_ssd = jnp.concatenate([cols(o, o + 1544), zpad(LANES - 8)], axis=-1).astype(BF16)
    w_gate = cols(OFF_GATE, OFF_GATE + N_BRANCH * d).reshape(depth, d, N_BRANCH, d)
    w_gate = jnp.transpose(w_gate, (0, 2, 1, 3)).astype(BF16)

    ng = _row(mix_norm_g)
    zl = jnp.zeros_like(rwkv_w_up)
    rwkv_lora = jnp.concatenate([jnp.concatenate([rwkv_w_up, zl], axis=-1),
                                 jnp.concatenate([zl, rwkv_a_up], axis=-1)], axis=1).astype(BF16)
    rwkv_params = [ng, w_rwkv, _row(rwkv_mu), _row(rwkv_w0), _row(rwkv_a0), rwkv_lora,
                   rwkv_g_up.astype(BF16), _row(rwkv_k_k), _row(rwkv_k_a),
                   _row(rwkv_r_k.reshape(depth, MIX)), _row(rwkv_gn_g), _row(rwkv_gn_b)]
    rwkv_consts = [jnp.asarray(_block_diag_ones(2 * LANES, 64), BF16), _chunk_selector(tt, RWKV_CHUNK)]
    rwkv_scratch = ([_vm((bsz, MIX // LANES, LANES, LANES)), _vm((bsz, CARRY_ROWS, RWKV_IN))]
                    + [_vm((rt, MIX), BF16)] * 7 + [_vm((rt, MIX))] * 4)

    gla_fup = jnp.pad(gla_f_up, ((0, 0), (0, LANES - gla_f_up.shape[1]), (0, 0))).astype(BF16)
    gla_params = [ng, w_gla, gla_fup, _row(gla_f_bias), _row(gla_norm_g)]
    gla_consts = [_chunk_selector(tt, GLA_CHUNK)]
    gla_scratch = [_vm((bsz, MIX, 256)), _vm((rt, 256)), _vm((rt, 256), BF16), _vm((rt, 256)),
                   _vm((rt, MIX), BF16), _vm((rt, 256)), _vm((rt, MIX)), _vm((rt, MIX))]

    zh = jnp.zeros_like(mlstm_i_bias)
    mlstm_params = [ng, w_mlstm, mlstm_conv_w, _row(mlstm_conv_b),
                    _row(_pad_lanes(jnp.concatenate([mlstm_i_bias, zh], axis=-1))),
                    _row(_pad_lanes(jnp.concatenate([zh, mlstm_f_bias], axis=-1))),
                    _row(mlstm_norm_g)]
    tile_of_lane = np.arange(MIX) // LANES
    lane_to_tile = lambda off: jnp.asarray(np.arange(LANES)[:, None] == off + tile_of_lane[None, :], BF16)
    mlstm_consts = [_chunk_selector(tt, MLSTM_CHUNK), lane_to_tile(0), lane_to_tile(4)]
    mlstm_scratch = [_vm((bsz, 256, 256)), _vm((bsz, 256, LANES)), _vm((bsz, CARRY_ROWS, 512)),
                     _vm((rt, 256), BF16), _vm((rt, 256)), _vm((rt, MIX), BF16), _vm((rt, MIX)),
                     _vm((rt, MIX)), _vm((rt, LANES)), _vm((rt, MIX)), _vm((rt, MIX))]

    ssd_params = [ng, w_ssd, ssd_conv_w, _row(ssd_conv_b), _row(_pad_lanes(ssd_dt_bias)),
                  _row(_pad_lanes(ssd_a_log)), _row(jnp.repeat(ssd_d, 64, axis=-1)), _row(ssd_norm_g)]
    hid = np.arange(MIX) // 64
    expand = jnp.asarray(np.arange(LANES)[:, None] == hid[None, :], BF16)
    ssd_consts = [_chunk_selector(tt, SSD_CHUNK), expand]
    ssd_scratch = [_vm((bsz, MIX // LANES, LANES, LANES)), _vm((bsz, CARRY_ROWS, 1024))]

    bproj = branch_proj.astype(BF16)
    wout = w_out.astype(BF16)
    fng = _row(ffn_norm_g)
    fup = ffn_up.astype(BF16)
    fdn = ffn_down.astype(BF16)
    fcb = _row(ffn_conv_b)

    m = bsz * t
    tm = 512
    tf = 2 * LANES
    assert t % tm == 0 and D_FF % tf == 0
    cparams = pltpu.CompilerParams(dimension_semantics=("arbitrary",), vmem_limit_bytes=VMEM_LIMIT)
    row_spec = pl.BlockSpec((tm, d), lambda i: (i, 0))
    br_spec = pl.BlockSpec((tm, MIX), lambda i: (i, 0))

    def resident(arr, l):
        tail = arr.shape[1:]
        return pl.BlockSpec((None,) + tail, lambda i, l=l, n=len(tail): (l,) + (0,) * n,
                            pipeline_mode=pl.Buffered(1))

    def merge(h2, l, branches):
        return pl.pallas_call(
            _merge_kernel,
            grid=(m // tm,),
            in_specs=[row_spec, resident(ng, l), resident(w_gate, l), br_spec, br_spec, br_spec, br_spec,
                      resident(bproj, l), resident(wout, l)],
            out_specs=row_spec,
            out_shape=jax.ShapeDtypeStruct((m, d), F32),
            compiler_params=cparams,
            name="merge",
        )(h2, ng, w_gate, *branches, bproj, wout)

    def ffn(h2, l):
        return pl.pallas_call(
            functools.partial(_ffn_kernel, tm=tm, tiles_per_seq=t // tm, tf=tf),
            grid=(m // tm,),
            in_specs=[row_spec, resident(fng, l), resident(fup, l), resident(ffn_conv_w, l), resident(fcb, l),
                      resident(fdn, l)],
            out_specs=row_spec,
            out_shape=jax.ShapeDtypeStruct((m, d), F32),
            scratch_shapes=[_vm((tm, D_FF), BF16), _vm((CARRY_ROWS, 2 * D_FF))],
            compiler_params=cparams,
            name="ffn",
        )(h2, fng, fup, ffn_conv_w, fcb, fdn)

    h = x
    for l in range(depth):
        branches = [
            _mixer_call(_rwkv_kernel, "rwkv", h, l, rwkv_params, rwkv_consts, rwkv_scratch, tt),
            _mixer_call(_gla_kernel, "gla", h, l, gla_params, gla_consts, gla_scratch, tt),
            _mixer_call(_mlstm_kernel, "mlstm", h, l, mlstm_params, mlstm_consts, mlstm_scratch, tt),
            _mixer_call(_ssd_kernel, "ssd", h, l, ssd_params, ssd_consts, ssd_scratch, tt),
        ]
        h2 = merge(h.reshape(m, d), l, [b.reshape(m, MIX) for b in branches])
        h = ffn(h2, l).reshape(bsz, t, d)

    out = pl.pallas_call(
        _norm_kernel,
        grid=(m // tm,),
        in_specs=[pl.BlockSpec((tm, d), lambda i: (i, 0)), pl.BlockSpec((1, d), lambda i: (0, 0))],
        out_specs=pl.BlockSpec((tm, d), lambda i: (i, 0)),
        out_shape=jax.ShapeDtypeStruct((m, d), x.dtype),
        name="final_norm",
    )(h.reshape(m, d), final_norm_g[None, :])
    return out.reshape(bsz, t, d)
```

```python
import functools

import numpy as np
import jax
import jax.numpy as jnp
from jax import lax
from jax.experimental import pallas as pl
from jax.experimental.pallas import tpu as pltpu

F32, BF16 = jnp.float32, jnp.bfloat16

D_MODEL = 1024
MIX = 512
EPS = 1e-6
N_BRANCH = 4
RWKV_DECAY_SCALE = 0.6065306597
RWKV_GN_EPS = 64e-5
RWKV_CHUNK = 64
GLA_CHUNK = 16
GLA_GROUP = 4
MLSTM_CHUNK = 64
SSD_CHUNK = 128
D_FF = 2816

RWKV_IN, GLA_IN, MLSTM_IN, SSD_IN = 1792, 1552, 1544, 1544
OFF_GLA = RWKV_IN
OFF_MLSTM = OFF_GLA + GLA_IN
OFF_SSD = OFF_MLSTM + MLSTM_IN
OFF_GATE = OFF_SSD + SSD_IN

LANES = 128
CARRY_ROWS = 8
MIXER_TILE = 128
VMEM_LIMIT = 48 * 1024 * 1024


def _mm(a, b):
    return jnp.dot(a.astype(BF16), b.astype(BF16), preferred_element_type=F32)


def _mm_nt(a, b):
    return lax.dot_general(a.astype(BF16), b.astype(BF16), (((1,), (1,)), ((), ())),
                           preferred_element_type=F32)


def _mm_tn(a, b):
    return lax.dot_general(a.astype(BF16), b.astype(BF16), (((0,), (0,)), ((), ())),
                           preferred_element_type=F32)


def _bf16_terms(x, n):
    terms, rest = [], x
    for _ in range(n):
        t = rest.astype(BF16)
        terms.append(t)
        rest = rest - t.astype(F32)
    return terms


def _sel_lhs(sel, x, n=3):
    out = None
    for t in _bf16_terms(x, n):
        p = jnp.dot(sel, t, preferred_element_type=F32)
        out = p if out is None else out + p
    return out


def _sel_rhs(x, sel, n=2):
    out = None
    for t in _bf16_terms(x, n):
        p = jnp.dot(t, sel, preferred_element_type=F32)
        out = p if out is None else out + p
    return out


def _chunk_sums(sel, x, nb, tt, n=2):
    parts = [_sel_lhs(sel, x[b * tt:(b + 1) * tt], n) for b in range(nb)]
    return (jnp.concatenate([p[:tt] for p in parts], axis=0),
            jnp.concatenate([p[tt:] for p in parts], axis=0))


def _rms(x, g):
    return x * lax.rsqrt(jnp.mean(x * x, axis=-1, keepdims=True) + EPS) * g


def _silu(x):
    return x * jax.nn.sigmoid(x)


def _softplus(x):
    return jnp.maximum(x, 0.0) + jnp.log1p(jnp.exp(-jnp.abs(x)))


def _log_sigmoid(x):
    return -_softplus(-x)


def _shift_rows(x, carry, s):
    y = pltpu.roll(x, s, 0)
    rows = lax.broadcasted_iota(jnp.int32, (CARRY_ROWS, 1), 0)
    head = y[:CARRY_ROWS]
    for q in range(s):
        src = CARRY_ROWS - s + q
        head = jnp.where(rows == q, carry[src:src + 1, :], head)
    return jnp.concatenate([head, y[CARRY_ROWS:]], axis=0)


def _causal_conv(x, carries, w, b, tt):
    k = w.shape[0]
    y = w[k - 1:k, :] * x + b
    for s in range(1, k):
        sh = jnp.concatenate([_shift_rows(x[i * tt:(i + 1) * tt], c, s) for i, c in enumerate(carries)],
                             axis=0)
        y = y + w[k - 1 - s:k - s, :] * sh
    return y


def _save_carry(carry_ref, x, nb, tt):
    for b in range(nb):
        carry_ref[b] = x[(b + 1) * tt - CARRY_ROWS:(b + 1) * tt, :]


def _first_step_zero(*refs):
    @pl.when(pl.program_id(0) == 0)
    def _init():
        for r in refs:
            r[...] = jnp.zeros_like(r)


def _rwkv_kernel(u_ref, w_ref, mu_ref, w0_ref, a0_ref, lora_ref, gup_ref, kk_ref, ka_ref,
                 rk_ref, gng_ref, gnb_ref, bd_ref, sel_ref, out_ref,
                 st_ref, prev_ref, ab_s, rb_s, bt_s, kt_s, v_s, bg_s, kg_s, gl_s, y_s, bon_s, g_s,
                 *, nb, tt):
    L = RWKV_CHUNK
    npair = MIX // LANES
    _first_step_zero(st_ref, prev_ref)

    u = u_ref[...].reshape(nb * tt, D_MODEL)
    fa = jnp.dot(u, w_ref[...], preferred_element_type=F32)
    prev = jnp.concatenate([_shift_rows(fa[b * tt:(b + 1) * tt], prev_ref[b], 1) for b in range(nb)], axis=0)
    _save_carry(prev_ref, fa, nb, tt)
    fa = fa + mu_ref[...] * (prev - fa)
    r, k, v = fa[:, :512], fa[:, 512:1024], fa[:, 1024:1536]
    lo = fa[:, 1536:1664]
    lane = lax.broadcasted_iota(jnp.int32, (1, LANES), 1)
    pre = _mm(jnp.where(lane < 64, jnp.tanh(lo), lo), lora_ref[...])
    lw = -RWKV_DECAY_SCALE * jax.nn.sigmoid(w0_ref[...] + pre[:, :512])
    a = jax.nn.sigmoid(a0_ref[...] + pre[:, 512:])
    g_s[...] = _mm(jax.nn.sigmoid(fa[:, 1664:1792]), gup_ref[...])
    bd = bd_ref[...]

    def head_sum(x):
        half = bd.shape[0]
        return jnp.concatenate([_sel_rhs(x[:, i:i + half], bd, 1) for i in range(0, MIX, half)], axis=1)

    kk = k * kk_ref[...]
    kk = kk * lax.rsqrt(jnp.maximum(head_sum(kk * kk), 1e-24))
    k = k * (1.0 + (a - 1.0) * ka_ref[...])
    bon_s[...] = head_sum(r * k * rk_ref[...]) * v
    b, bl = _chunk_sums(sel_ref[...], lw, nb, tt)
    eb, enb, et = jnp.exp(b), jnp.exp(-b), jnp.exp(bl - b)
    beta = kk * a
    ab_s[...] = (-kk * jnp.exp(b - lw)).astype(BF16)
    rb_s[...] = (r * eb).astype(BF16)
    bt_s[...] = (beta * enb).astype(BF16)
    kt_s[...] = (k * enb).astype(BF16)
    v_s[...] = v.astype(BF16)
    bg_s[...] = (beta * et).astype(BF16)
    kg_s[...] = (k * et).astype(BF16)
    gl_s[...] = jnp.exp(bl)

    ri = lax.broadcasted_iota(jnp.int32, (2 * L, 2 * L), 0)
    ci = lax.broadcasted_iota(jnp.int32, (2 * L, 2 * L), 1)
    same = (ri // L) == (ci // L)
    strict = jnp.logical_and(same, (ri % L) > (ci % L))
    incl = jnp.logical_and(same, (ri % L) >= (ci % L))
    m0 = lane < 64
    chains = [(bi, p) for bi in range(nb) for p in range(npair)]

    def body(c, carry):
        r0 = c * L

        def ld(ref, bi, p):
            x = ref[pl.ds(bi * tt + r0, L), p * LANES:(p + 1) * LANES]
            zero = jnp.zeros_like(x)
            return jnp.concatenate([jnp.where(m0, x, zero), jnp.where(m0, zero, x)], axis=0)

        def ld2(ref_a, ref_b, bi, p):
            return jnp.concatenate([ld(ref_a, bi, p), ld(ref_b, bi, p)], axis=0)

        gm = [_mm_nt(ld2(ab_s, rb_s, bi, p), ld2(bt_s, kt_s, bi, p)) for bi, p in chains]
        a_ab = [jnp.where(strict, x[:2 * L, :2 * L], 0.0) for x in gm]
        a_ak = [jnp.where(strict, x[:2 * L, 2 * L:], 0.0).astype(BF16) for x in gm]
        a_rb = [jnp.where(incl, x[2 * L:, :2 * L], 0.0).astype(BF16) for x in gm]
        a_rk = [jnp.where(incl, x[2 * L:, 2 * L:], 0.0).astype(BF16) for x in gm]
        st = [st_ref[bi, p] for bi, p in chains]
        stb = [x.astype(BF16) for x in st]
        uu = [_mm_nt(ld(ab_s, bi, p), s) + _mm(ak, ld(v_s, bi, p))
              for (bi, p), s, ak in zip(chains, stb, a_ak)]
        pw = [x.astype(BF16) for x in a_ab]
        levels = int(np.log2(L))
        for lvl in range(levels):
            if lvl == levels - 1:
                uu = [x + _mm(p_, x) for p_, x in zip(pw, uu)]
            else:
                res = [_mm(p_, jnp.concatenate([p_, x.astype(BF16)], axis=1)) for p_, x in zip(pw, uu)]
                pw = [x[:, :2 * L].astype(BF16) for x in res]
                uu = [x + y[:, 2 * L:] for x, y in zip(uu, res)]
        uu = [x.astype(BF16) for x in uu]
        yy = [_mm_nt(ld(rb_s, bi, p), s) + _mm(rb, x) + _mm(rk, ld(v_s, bi, p))
              for (bi, p), s, rb, rk, x in zip(chains, stb, a_rb, a_rk, uu)]
        for (bi, p), y in zip(chains, yy):
            y_s[pl.ds(bi * tt + r0, L), p * LANES:(p + 1) * LANES] = y[:L] + y[L:]
        upd = [_mm_tn(jnp.concatenate([x, ld(v_s, bi, p)], axis=0), ld2(bg_s, kg_s, bi, p))
               for (bi, p), x in zip(chains, uu)]
        for (bi, p), s, d in zip(chains, st, upd):
            st_ref[bi, p] = s * gl_s[pl.ds(bi * tt + r0, 1), p * LANES:(p + 1) * LANES] + d
        return carry

    for c in range(tt // L):
        body(c, 0)

    y = y_s[...]
    yc = y - head_sum(y) * (1.0 / 64)
    var = head_sum(yc * yc) * (1.0 / 64)
    yn = yc * lax.rsqrt(var + RWKV_GN_EPS) * gng_ref[...] + gnb_ref[...]
    out_ref[...] = ((yn + bon_s[...]) * g_s[...]).astype(BF16).reshape(nb, tt, MIX)


def _gla_kernel(u_ref, w_ref, fup_ref, fb_ref, gn_ref, sel_ref, out_ref,
                st_ref, qd_s, kd_s, kg_s, v_s, bl_s, o_s, og_s, *, nb, tt):
    L = GLA_CHUNK
    nh = 4
    nc = GLA_GROUP
    _first_step_zero(st_ref)

    u = u_ref[...].reshape(nb * tt, D_MODEL)
    fb = jnp.dot(u, w_ref[...], preferred_element_type=F32)
    q, k = fb[:, :256] * 0.125, fb[:, 256:512]
    v_s[...] = fb[:, 512:1024].astype(BF16)
    og_s[...] = _silu(fb[:, 1024:1536])
    la = _log_sigmoid(_mm(fb[:, 1536:1664], fup_ref[...]) + fb_ref[...]) * (1.0 / 16.0)
    b, bl = _chunk_sums(sel_ref[...], la, nb, tt)
    qd_s[...] = q * jnp.exp(b)
    kd_s[...] = (k * jnp.exp(-b)).astype(BF16)
    kg_s[...] = k * jnp.exp(bl - b)
    bl_s[...] = bl

    hq = lax.broadcasted_iota(jnp.int32, (1, 256), 1) // 64
    hv = lax.broadcasted_iota(jnp.int32, (1, 512), 1) // 128
    def key_mask(c):
        ri = lax.broadcasted_iota(jnp.int32, (nh * L, nh * L * (c + 1)), 0)
        ci = lax.broadcasted_iota(jnp.int32, (nh * L, nh * L * (c + 1)), 1)
        own = jnp.logical_and((ri // L) == (ci // L), (ri % L) >= (ci % L))
        return jnp.logical_or(ci >= nh * L, own)

    masks = [key_mask(c) for c in range(nc)]
    seqs = range(nb)

    def stack(x, hid):
        x = x.astype(BF16)
        zero = jnp.zeros_like(x)
        return jnp.concatenate([jnp.where(hid == h, x, zero) for h in range(nh)], axis=0)

    def body(g, carry):
        r0 = g * (nc * L)
        keys, vals, qg, kgs, gtot = [], [], [], [], []
        for bi in seqs:
            rows = [pl.ds(bi * tt + r0 + L * c, L) for c in range(nc)]
            tot = [bl_s[pl.ds(bi * tt + r0 + L * c, 1), :] for c in range(nc)]

            def span(lo, hi):
                acc = None
                for c in range(lo, hi):
                    acc = tot[c] if acc is None else acc + tot[c]
                return 1.0 if acc is None else jnp.exp(acc)

            xq = [qd_s[r, :] for r in rows]
            xkg = [kg_s[r, :] for r in rows]
            xv = [stack(v_s[r, :], hv) for r in rows]
            keys.append([jnp.concatenate([stack(kd_s[rows[c], :], hq)]
                                         + [stack(xkg[j] * span(j + 1, c), hq) for j in range(c - 1, -1, -1)], axis=0)
                         for c in range(nc)])
            vals.append([jnp.concatenate([xv[c]] + [xv[j] for j in range(c - 1, -1, -1)], axis=0) for c in range(nc)])
            qg.append([stack(xq[c], hq) for c in range(nc)]
                      + [jnp.concatenate([stack(xq[c] * span(0, c), hq) for c in range(nc)], axis=0)])
            kgs.append((jnp.concatenate(xv, axis=0),
                        jnp.concatenate([stack(xkg[c] * span(c + 1, nc), hq) for c in range(nc)], axis=0)))
            gtot.append(span(0, nc))
        att = [[jnp.where(masks[c], _mm_nt(qg[bi][c], keys[bi][c]), 0.0) for c in range(nc)] for bi in seqs]
        upd = [_mm_tn(*kgs[bi]) for bi in seqs]
        st = [st_ref[bi] for bi in seqs]
        oi = [_mm_nt(qg[bi][nc], st[bi]) for bi in seqs]
        for bi in seqs:
            for c in range(nc):
                oo = _mm(att[bi][c], vals[bi][c]) + oi[bi][c * nh * L:(c + 1) * nh * L]
                acc = oo[:L]
                for h in range(1, nh):
                    acc = acc + oo[h * L:(h + 1) * L]
                o_s[pl.ds(bi * tt + r0 + L * c, L), :] = acc
            st_ref[bi] = st[bi] * gtot[bi] + upd[bi]
        return carry

    for g in range(tt // (nc * L)):
        body(g, 0)

    for h in range(nh):
        cs = slice(h * LANES, (h + 1) * LANES)
        oh = o_s[:, cs]
        oh = oh * lax.rsqrt(jnp.mean(oh * oh, axis=-1, keepdims=True) + EPS) * gn_ref[...]
        out_ref[:, :, cs] = (oh * og_s[:, cs]).astype(BF16).reshape(nb, tt, LANES)


def _mlstm_kernel(u_ref, w_ref, cw_ref, cb_ref, ib_ref, fbias_ref, gn_ref, sel_ref, eli_ref, eb_ref,
                  out_ref, c_ref, m_ref, prev_ref, q_s, k_s, v_s, bf_s, lf_s, lib_s, h_s, og_s, *, nb, tt):
    L = MLSTM_CHUNK
    nh = 4
    _first_step_zero(c_ref, m_ref, prev_ref)

    u = u_ref[...].reshape(nb * tt, D_MODEL)
    fc = jnp.dot(u, w_ref[...], preferred_element_type=F32)
    qk = fc[:, :512]
    conv = _causal_conv(qk, [prev_ref[b] for b in range(nb)], cw_ref[...], cb_ref[...], tt)
    _save_carry(prev_ref, qk, nb, tt)
    qk = _silu(conv)
    q_s[...] = qk[:, :256].astype(BF16)
    k_s[...] = qk[:, 256:] * 0.125
    v_s[...] = fc[:, 512:1024].astype(BF16)
    og_s[...] = jax.nn.sigmoid(fc[:, 1024:1536])
    misc = fc[:, 1536:1664]
    lane = lax.broadcasted_iota(jnp.int32, (1, LANES), 1)
    li = misc + ib_ref[...]
    b = _chunk_sums(sel_ref[...], _log_sigmoid(misc + fbias_ref[...]), nb, tt)[0]
    lf_s[...] = _sel_rhs(li, eli_ref[...], 2)
    bf_s[...] = _sel_rhs(b, eb_ref[...], 3)
    lib_s[...] = jnp.where(lane < nh, li, jnp.where(lane < 2 * nh, -b, 0.0))

    hq = lax.broadcasted_iota(jnp.int32, (1, 256), 1) // 64
    ri = lax.broadcasted_iota(jnp.int32, (nh * L, nh * L), 0)
    ci = lax.broadcasted_iota(jnp.int32, (nh * L, nh * L), 1)
    dmask = jnp.logical_and((ri // L) == (ci // L), (ri % L) >= (ci % L))
    ones = jnp.ones((L, LANES), BF16)
    ones_rows = jnp.ones((nh * L, LANES), BF16)
    seqs = range(nb)

    def stackq(x):
        zero = jnp.zeros_like(x)
        return jnp.concatenate([jnp.where(hq == h, x, zero) for h in range(nh)], axis=0)

    def headcol(ref, rs):
        x = ref[rs, :]
        return jnp.concatenate([x[:, h * LANES:(h + 1) * LANES] for h in range(nh)], axis=0)

    def perhead_last(col):
        return jnp.concatenate(
            [jnp.broadcast_to(col[(h + 1) * L - 1:(h + 1) * L, :], (L, LANES)) for h in range(nh)], axis=0)

    def wide(col):
        return jnp.concatenate([col, col], axis=1)

    def body(c, carry):
        r0 = c * L
        rows = [pl.ds(bi * tt + r0, L) for bi in seqs]
        xq = [stackq(q_s[rs, :]) for rs in rows]
        kf = [stackq(k_s[rs, :]) for rs in rows]
        xv = []
        for rs in rows:
            vc = v_s[rs, :]
            xv.append(jnp.concatenate(
                [jnp.concatenate([vc[:, h * LANES:(h + 1) * LANES], ones], axis=1) for h in range(nh)], axis=0))
        qk = [_mm_nt(xq[bi], kf[bi]) for bi in seqs]
        qc = [_mm(xq[bi], c_ref[bi]) for bi in seqs]
        drow = []
        for rs in rows:
            x = lib_s[rs, :]
            zx = jnp.concatenate(
                [jnp.where(jnp.logical_or(lane == h, lane == nh + h), x, 0.0) for h in range(nh)], axis=0)
            acc = None
            for t in _bf16_terms(zx, 3):
                p = lax.dot_general(ones_rows, t, (((1,), (1,)), ((), ())), preferred_element_type=F32)
                acc = p if acc is None else acc + p
            drow.append(acc)
        for bi in seqs:
            rs = rows[bi]
            bcol, licol = headcol(bf_s, rs), headcol(lf_s, rs)
            mcol = m_ref[bi]
            log_d = jnp.where(dmask, wide(bcol) + drow[bi], -jnp.inf)
            mx = jnp.broadcast_to(jnp.max(log_d, axis=-1, keepdims=True), (nh * L, LANES))
            m_t = jnp.maximum(bcol + mcol, mx)
            d = jnp.exp(log_d - wide(m_t))
            inter = jnp.exp(bcol + mcol - m_t)
            num = _mm(qk[bi] * d, xv[bi]) + qc[bi] * wide(inter)
            hh = num[:, :LANES] / jnp.maximum(jnp.abs(num[:, LANES:]), jnp.exp(-m_t))
            for h in range(nh):
                h_s[rs, h * LANES:(h + 1) * LANES] = hh[h * L:(h + 1) * L, :]
            m_new = perhead_last(m_t)
            w_last = jnp.exp(perhead_last(bcol) - bcol + licol - m_new)
            c_ref[bi] = wide(perhead_last(inter)) * c_ref[bi] + _mm_tn(kf[bi] * wide(w_last), xv[bi])
            m_ref[bi] = m_new
        return carry

    for c in range(tt // L):
        body(c, 0)

    for h in range(nh):
        cs = slice(h * LANES, (h + 1) * LANES)
        hh = h_s[:, cs]
        hh = hh * lax.rsqrt(jnp.mean(hh * hh, axis=-1, keepdims=True) + EPS) * gn_ref[:, cs]
        out_ref[:, :, cs] = (hh * og_s[:, cs]).astype(BF16).reshape(nb, tt, LANES)


def _ssd_kernel(u_ref, w_ref, cw_ref, cb_ref, dtb_ref, alog_ref, dsk_ref, gn_ref, sel_ref,
                ex_ref, out_ref, st_ref, prev_ref, *, nb, tt):
    L = SSD_CHUNK
    assert tt == L
    _first_step_zero(st_ref, prev_ref)

    u = u_ref[...].reshape(nb * tt, D_MODEL)
    fd = jnp.dot(u, w_ref[...], preferred_element_type=F32)
    z = _silu(fd[:, :512])
    xbc = fd[:, 512:1536]
    conv = _silu(_causal_conv(xbc, [prev_ref[b] for b in range(nb)], cw_ref[...], cb_ref[...], tt))
    _save_carry(prev_ref, xbc, nb, tt)
    x = conv[:, :512]
    bm = conv[:, 512:768].astype(BF16)
    cm = conv[:, 768:1024].astype(BF16)
    lane = lax.broadcasted_iota(jnp.int32, (1, LANES), 1)
    valid = lane < 8
    dt = jnp.where(valid, _softplus(fd[:, 1536:1664] + dtb_ref[...]), 0.0)
    dta = dt * jnp.where(valid, -jnp.exp(alog_ref[...]), 0.0)
    ac = _chunk_sums(sel_ref[...], dta, nb, tt, 3)[0]
    ace = _sel_rhs(ac, ex_ref[...], 3)
    xd = x * _sel_rhs(dt, ex_ref[...], 2)

    ri = lax.broadcasted_iota(jnp.int32, (L, L), 0)
    ci = lax.broadcasted_iota(jnp.int32, (L, L), 1)
    tril = ri >= ci
    m0 = lane < 64
    npair = MIX // LANES
    seqs = range(nb)
    seg = lambda a, bi: a[bi * tt:(bi + 1) * tt]
    groups = [(bi, g) for bi in seqs for g in range(2)]
    chains = [(bi, p) for bi in seqs for p in range(npair)]

    cb = {(bi, g): _mm_nt(seg(cm, bi)[:, g * LANES:(g + 1) * LANES], seg(bm, bi)[:, g * LANES:(g + 1) * LANES])
          for bi, g in groups}
    last = {bi: seg(ace, bi)[L - 1:L, :] for bi in seqs}
    ys = {}
    for bi, p in chains:
        cs = slice(p * LANES, (p + 1) * LANES)
        gs = slice((p // 2) * LANES, (p // 2 + 1) * LANES)
        ys[bi, p] = _mm(seg(cm, bi)[:, gs], st_ref[bi, p]) * jnp.exp(seg(ace, bi)[:, cs])
    for bi, p in chains:
        cs = slice(p * LANES, (p + 1) * LANES)
        gs = slice((p // 2) * LANES, (p // 2 + 1) * LANES)
        xdec = seg(xd, bi)[:, cs] * jnp.exp(last[bi][:, cs] - seg(ace, bi)[:, cs])
        st_ref[bi, p] = st_ref[bi, p] * jnp.exp(last[bi][:, cs]) + _mm_tn(seg(bm, bi)[:, gs], xdec)
    for bi in seqs:
        acc = seg(ac, bi)
        act = acc.T
        for p in range(npair):
            cs = slice(p * LANES, (p + 1) * LANES)
            xp = seg(xd, bi)[:, cs]
            y = ys[bi, p]
            for hh, xm in ((2 * p, jnp.where(m0, xp, 0.0)), (2 * p + 1, jnp.where(m0, 0.0, xp))):
                sg = jnp.exp(jnp.where(tril, acc[:, hh:hh + 1] - act[hh:hh + 1, :], -jnp.inf))
                y = y + _mm(cb[bi, p // 2] * sg, xm)
            ys[bi, p] = y
    y = jnp.concatenate([jnp.concatenate([ys[bi, p] for p in range(npair)], axis=1) for bi in seqs], axis=0)
    y = (y + x * dsk_ref[...]) * z
    out_ref[...] = _rms(y, gn_ref[...]).astype(BF16).reshape(nb, tt, MIX)


def _merge_kernel(h_ref, u_ref, wg_ref, b0_ref, b1_ref, b2_ref, b3_ref, p_ref, wo_ref, gnext_ref,
                  out_ref, unext_ref):
    u = u_ref[...]
    acc = None
    for i, br in enumerate((b0_ref, b1_ref, b2_ref, b3_ref)):
        gate = jax.nn.sigmoid(jnp.dot(u, wg_ref[:, i * D_MODEL:(i + 1) * D_MODEL], preferred_element_type=F32))
        term = gate * jnp.dot(br[...], p_ref[i], preferred_element_type=F32)
        acc = term if acc is None else acc + term
    h = h_ref[...] + _mm(acc, wo_ref[...])
    out_ref[...] = h
    unext_ref[...] = _rms(h, gnext_ref[...]).astype(BF16)


def _ffn_kernel(h_ref, u_ref, up_ref, cw_ref, cb_ref, dn_ref, gnext_ref, *rest, tm, tiles_per_seq, tf, last):
    out_refs, (a_s, carry_s) = rest[:-2], rest[-2:]
    u = u_ref[...]
    seq_start = pl.program_id(0) % tiles_per_seq == 0

    def half(off):
        cs = slice(off, off + tf)
        z = jnp.dot(u, up_ref[:, cs], preferred_element_type=F32)
        carry = jnp.where(seq_start, 0.0, carry_s[:, cs])
        carry_s[:, cs] = z[tm - CARRY_ROWS:, :]
        return _causal_conv(z, [carry], cw_ref[:, cs], cb_ref[:, cs], tm)

    for j in range(D_FF // tf):
        gate, val = half(j * tf), half(D_FF + j * tf)
        a_s[:, j * tf:(j + 1) * tf] = (_silu(gate) * val).astype(BF16)
    h = h_ref[...] + jnp.dot(a_s[...], dn_ref[...], preferred_element_type=F32)
    if last:
        out_refs[0][...] = _rms(h, gnext_ref[...])
    else:
        out_refs[0][...] = h
        out_refs[1][...] = _rms(h, gnext_ref[...]).astype(BF16)


def _norm_kernel(h_ref, g_ref, out_ref):
    out_ref[...] = _rms(h_ref[...], g_ref[...]).astype(out_ref.dtype)


def _layer_spec(arr, l):
    tail = arr.shape[1:]
    return pl.BlockSpec((None,) + tail, lambda *_, l=l, n=len(tail): (l,) + (0,) * n)


def _full_spec(arr):
    return pl.BlockSpec(arr.shape, lambda *_, n=arr.ndim: (0,) * n)


def _mixer_call(kern, name, h, l, layer_params, consts, scratch, tt):
    bsz, t, d = h.shape
    in_specs = [pl.BlockSpec((bsz, tt, d), lambda i: (0, i, 0))]
    in_specs += [_layer_spec(a, l) for a in layer_params]
    in_specs += [_full_spec(a) for a in consts]
    return pl.pallas_call(
        functools.partial(kern, nb=bsz, tt=tt),
        grid=(t // tt,),
        in_specs=in_specs,
        out_specs=pl.BlockSpec((bsz, tt, MIX), lambda i: (0, i, 0)),
        out_shape=jax.ShapeDtypeStruct((bsz, t, MIX), BF16),
        scratch_shapes=scratch,
        compiler_params=pltpu.CompilerParams(dimension_semantics=("arbitrary",),
                                             vmem_limit_bytes=VMEM_LIMIT),
        name=name,
    )(h, *layer_params, *consts)


def _vm(shape, dtype=F32):
    return pltpu.VMEM(shape, dtype)


def _block_diag_ones(n, blk, lower=False):
    i = np.arange(n)
    m = (i[:, None] // blk) == (i[None, :] // blk)
    if lower:
        m = m & (i[:, None] >= i[None, :])
    return m


def _chunk_selector(tt, chunk):
    return jnp.asarray(np.concatenate([_block_diag_ones(tt, chunk, lower=True), _block_diag_ones(tt, chunk)]), BF16)


def _pad_lanes(a, width=LANES):
    return jnp.pad(a, [(0, 0)] * (a.ndim - 1) + [(0, width - a.shape[-1])])


def _row(a):
    return a[:, None, :]


def kernel(x, mix_norm_g, w_in, rwkv_mu, rwkv_w0, rwkv_w_up, rwkv_a0, rwkv_a_up, rwkv_g_up, rwkv_k_k, rwkv_k_a, rwkv_r_k, rwkv_gn_g, rwkv_gn_b, gla_f_up, gla_f_bias, gla_norm_g, mlstm_conv_w, mlstm_conv_b, mlstm_i_bias, mlstm_f_bias, mlstm_norm_g, ssd_conv_w, ssd_conv_b, ssd_dt_bias, ssd_a_log, ssd_d, ssd_norm_g, branch_proj, w_out, ffn_norm_g, ffn_up, ffn_conv_w, ffn_conv_b, ffn_down, final_norm_g):
    bsz, t, d = x.shape
    depth = w_in.shape[0]
    tt = MIXER_TILE
    rt = bsz * tt
    assert d == D_MODEL and t % tt == 0

    def cols(lo, hi):
        return w_in[:, :, lo:hi]

    zpad = lambda n: jnp.zeros((depth, d, n), F32)
    w_rwkv = cols(0, RWKV_IN).astype(BF16)
    o = OFF_GLA
    w_gla = jnp.concatenate([cols(o, o + 1024), cols(o + 1040, o + 1552), cols(o + 1024, o + 1040),
                             zpad(LANES - 16)], axis=-1).astype(BF16)
    o = OFF_MLSTM
    w_mlstm = jnp.concatenate([cols(o, o + 1024), cols(o + 1032, o + 1544), cols(o + 1024, o + 1032),
                               zpad(LANES - 8)], axis=-1).astype(BF16)
    o = OFF_SSD
    w_ssd = jnp.concatenate([cols(o, o + 1544), zpad(LANES - 8)], axis=-1).astype(BF16)
    w_gate = cols(OFF_GATE, OFF_GATE + N_BRANCH * d).astype(BF16)

    ng = _row(mix_norm_g)
    zl = jnp.zeros_like(rwkv_w_up)
    rwkv_lora = jnp.concatenate([jnp.concatenate([rwkv_w_up, zl], axis=-1),
                                 jnp.concatenate([zl, rwkv_a_up], axis=-1)], axis=1).astype(BF16)
    rwkv_params = [w_rwkv, _row(rwkv_mu), _row(rwkv_w0), _row(rwkv_a0), rwkv_lora,
                   rwkv_g_up.astype(BF16), _row(rwkv_k_k), _row(rwkv_k_a),
                   _row(rwkv_r_k.reshape(depth, MIX)), _row(rwkv_gn_g), _row(rwkv_gn_b)]
    rwkv_consts = [jnp.asarray(_block_diag_ones(2 * LANES, 64), BF16), _chunk_selector(tt, RWKV_CHUNK)]
    rwkv_scratch = ([_vm((bsz, MIX // LANES, LANES, LANES)), _vm((bsz, CARRY_ROWS, RWKV_IN))]
                    + [_vm((rt, MIX), BF16)] * 7 + [_vm((rt, MIX))] * 4)

    gla_fup = jnp.pad(gla_f_up, ((0, 0), (0, LANES - gla_f_up.shape[1]), (0, 0))).astype(BF16)
    gla_params = [w_gla, gla_fup, _row(gla_f_bias), _row(gla_norm_g)]
    gla_consts = [_chunk_selector(tt, GLA_CHUNK)]
    gla_scratch = [_vm((bsz, MIX, 256)), _vm((rt, 256)), _vm((rt, 256), BF16), _vm((rt, 256)),
                   _vm((rt, MIX), BF16), _vm((rt, 256)), _vm((rt, MIX)), _vm((rt, MIX))]

    zh = jnp.zeros_like(mlstm_i_bias)
    mlstm_params = [w_mlstm, mlstm_conv_w, _row(mlstm_conv_b),
                    _row(_pad_lanes(jnp.concatenate([mlstm_i_bias, zh], axis=-1))),
                    _row(_pad_lanes(jnp.concatenate([zh, mlstm_f_bias], axis=-1))),
                    _row(mlstm_norm_g)]
    tile_of_lane = np.arange(MIX) // LANES
    lane_to_tile = lambda off: jnp.asarray(np.arange(LANES)[:, None] == off + tile_of_lane[None, :], BF16)
    mlstm_consts = [_chunk_selector(tt, MLSTM_CHUNK), lane_to_tile(0), lane_to_tile(4)]
    mlstm_scratch = [_vm((bsz, 256, 256)), _vm((bsz, 256, LANES)), _vm((bsz, CARRY_ROWS, 512)),
                     _vm((rt, 256), BF16), _vm((rt, 256)), _vm((rt, MIX), BF16), _vm((rt, MIX)),
                     _vm((rt, MIX)), _vm((rt, LANES)), _vm((rt, MIX)), _vm((rt, MIX))]

    ssd_params = [w_ssd, ssd_conv_w, _row(ssd_conv_b), _row(_pad_lanes(ssd_dt_bias)),
                  _row(_pad_lanes(ssd_a_log)), _row(jnp.repeat(ssd_d, 64, axis=-1)), _row(ssd_norm_g)]
    hid = np.arange(MIX) // 64
    expand = jnp.asarray(np.arange(LANES)[:, None] == hid[None, :], BF16)
    ssd_consts = [_chunk_selector(tt, SSD_CHUNK), expand]
    ssd_scratch = [_vm((bsz, MIX // LANES, LANES, LANES)), _vm((bsz, CARRY_ROWS, 1024))]

    bproj = branch_proj.astype(BF16)
    wout = w_out.astype(BF16)
    fng = _row(ffn_norm_g)
    fup = ffn_up.astype(BF16)
    fdn = ffn_down.astype(BF16)
    fcb = _row(ffn_conv_b)

    m = bsz * t
    tm = 512
    tf = 2 * LANES
    assert t % tm == 0 and D_FF % tf == 0
    cparams = pltpu.CompilerParams(dimension_semantics=("arbitrary",), vmem_limit_bytes=VMEM_LIMIT)
    row_spec = pl.BlockSpec((tm, d), lambda i: (i, 0))
    br_spec = pl.BlockSpec((tm, MIX), lambda i: (i, 0))

    def resident(arr, l):
        tail = arr.shape[1:]
        return pl.BlockSpec((None,) + tail, lambda i, l=l, n=len(tail): (l,) + (0,) * n,
                            pipeline_mode=pl.Buffered(1))

    def merge(h2, u2, l, branches):
        return pl.pallas_call(
            _merge_kernel,
            grid=(m // tm,),
            in_specs=[row_spec, row_spec, resident(w_gate, l), br_spec, br_spec, br_spec, br_spec,
                      resident(bproj, l), resident(wout, l), resident(fng, l)],
            out_specs=[row_spec, row_spec],
            out_shape=[jax.ShapeDtypeStruct((m, d), F32), jax.ShapeDtypeStruct((m, d), BF16)],
            compiler_params=cparams,
            name="merge",
        )(h2, u2, w_gate, *branches, bproj, wout, fng)

    def ffn(h2, u2, l):
        last = l == depth - 1
        if last:
            gnext, gspec = final_norm_g[None, :], pl.BlockSpec((1, d), lambda i: (0, 0))
            out_specs, out_shape = row_spec, jax.ShapeDtypeStruct((m, d), x.dtype)
        else:
            gnext, gspec = ng, resident(ng, l + 1)
            out_specs = [row_spec, row_spec]
            out_shape = [jax.ShapeDtypeStruct((m, d), F32), jax.ShapeDtypeStruct((m, d), BF16)]
        return pl.pallas_call(
            functools.partial(_ffn_kernel, tm=tm, tiles_per_seq=t // tm, tf=tf, last=last),
            grid=(m // tm,),
            in_specs=[row_spec, row_spec, resident(fup, l), resident(ffn_conv_w, l), resident(fcb, l),
                      resident(fdn, l), gspec],
            out_specs=out_specs,
            out_shape=out_shape,
            scratch_shapes=[_vm((tm, D_FF), BF16), _vm((CARRY_ROWS, 2 * D_FF))],
            compiler_params=cparams,
            name="ffn",
        )(h2, u2, fup, ffn_conv_w, fcb, fdn, gnext)

    h = x.reshape(m, d)
    u = pl.pallas_call(
        _norm_kernel,
        grid=(m // tm,),
        in_specs=[row_spec, resident(ng, 0)],
        out_specs=row_spec,
        out_shape=jax.ShapeDtypeStruct((m, d), BF16),
        name="first_norm",
    )(h, ng)
    for l in range(depth):
        u3 = u.reshape(bsz, t, d)
        branches = [
            _mixer_call(_rwkv_kernel, "rwkv", u3, l, rwkv_params, rwkv_consts, rwkv_scratch, tt),
            _mixer_call(_gla_kernel, "gla", u3, l, gla_params, gla_consts, gla_scratch, tt),
            _mixer_call(_mlstm_kernel, "mlstm", u3, l, mlstm_params, mlstm_consts, mlstm_scratch, tt),
            _mixer_call(_ssd_kernel, "ssd", u3, l, ssd_params, ssd_consts, ssd_scratch, tt),
        ]
        h, u = merge(h, u, l, [b.reshape(m, MIX) for b in branches])
        if l == depth - 1:
            h = ffn(h, u, l)
        else:
            h, u = ffn(h, u, l)
    return h.reshape(bsz, t, d)
```

```python
import functools

import numpy as np
import jax
import jax.numpy as jnp
from jax import lax
from jax.experimental import pallas as pl
from jax.experimental.pallas import tpu as pltpu

F32, BF16 = jnp.float32, jnp.bfloat16

D_MODEL = 1024
MIX = 512
EPS = 1e-6
N_BRANCH = 4
RWKV_DECAY_SCALE = 0.6065306597
RWKV_GN_EPS = 64e-5
RWKV_CHUNK = 64
GLA_CHUNK = 16
GLA_GROUP = 4
MLSTM_CHUNK = 64
SSD_CHUNK = 128
D_FF = 2816

RWKV_IN, GLA_IN, MLSTM_IN, SSD_IN = 1792, 1552, 1544, 1544
OFF_GLA = RWKV_IN
OFF_MLSTM = OFF_GLA + GLA_IN
OFF_SSD = OFF_MLSTM + MLSTM_IN
OFF_GATE = OFF_SSD + SSD_IN

LANES = 128
CARRY_ROWS = 8
MIXER_TILE = 128
VMEM_LIMIT = 48 * 1024 * 1024


def _mm(a, b):
    return jnp.dot(a.astype(BF16), b.astype(BF16), preferred_element_type=F32)


def _mm_nt(a, b):
    return lax.dot_general(a.astype(BF16), b.astype(BF16), (((1,), (1,)), ((), ())),
                           preferred_element_type=F32)


def _mm_tn(a, b):
    return lax.dot_general(a.astype(BF16), b.astype(BF16), (((0,), (0,)), ((), ())),
                           preferred_element_type=F32)


def _bf16_terms(x, n):
    terms, rest = [], x
    for _ in range(n):
        t = rest.astype(BF16)
        terms.append(t)
        rest = rest - t.astype(F32)
    return terms


def _sel_lhs(sel, x, n=3):
    out = None
    for t in _bf16_terms(x, n):
        p = jnp.dot(sel, t, preferred_element_type=F32)
        out = p if out is None else out + p
    return out


def _sel_rhs(x, sel, n=2):
    out = None
    for t in _bf16_terms(x, n):
        p = jnp.dot(t, sel, preferred_element_type=F32)
        out = p if out is None else out + p
    return out


def _chunk_cumsum(sel, x, nb, tt, n=2):
    return jnp.concatenate([_sel_lhs(sel, x[b * tt:(b + 1) * tt], n) for b in range(nb)], axis=0)


def _chunk_last(b, chunk):
    return jnp.concatenate(
        [jnp.broadcast_to(b[i + chunk - 1:i + chunk, :], (chunk, b.shape[1])) for i in range(0, b.shape[0], chunk)],
        axis=0)


def _rms(x, g):
    return x * lax.rsqrt(jnp.mean(x * x, axis=-1, keepdims=True) + EPS) * g


def _silu(x):
    return x * jax.nn.sigmoid(x)


def _softplus(x):
    return jnp.maximum(x, 0.0) + jnp.log1p(jnp.exp(-jnp.abs(x)))


def _log_sigmoid(x):
    return -_softplus(-x)


def _shift_rows(x, carry, s):
    y = pltpu.roll(x, s, 0)
    rows = lax.broadcasted_iota(jnp.int32, (CARRY_ROWS, 1), 0)
    head = y[:CARRY_ROWS]
    for q in range(s):
        src = CARRY_ROWS - s + q
        head = jnp.where(rows == q, carry[src:src + 1, :], head)
    return jnp.concatenate([head, y[CARRY_ROWS:]], axis=0)


def _causal_conv(x, carries, w, b, tt):
    k = w.shape[0]
    y = w[k - 1:k, :] * x + b
    for s in range(1, k):
        sh = jnp.concatenate([_shift_rows(x[i * tt:(i + 1) * tt], c, s) for i, c in enumerate(carries)],
                             axis=0)
        y = y + w[k - 1 - s:k - s, :] * sh
    return y


def _save_carry(carry_ref, x, nb, tt):
    for b in range(nb):
        carry_ref[b] = x[(b + 1) * tt - CARRY_ROWS:(b + 1) * tt, :]


def _first_step_zero(*refs):
    @pl.when(pl.program_id(0) == 0)
    def _init():
        for r in refs:
            r[...] = jnp.zeros_like(r)


def _rwkv_kernel(u_ref, w_ref, mu_ref, w0_ref, a0_ref, lora_ref, gup_ref, kk_ref, ka_ref,
                 rk_ref, gng_ref, gnb_ref, bd_ref, sel_ref, out_ref,
                 st_ref, prev_ref, ab_s, rb_s, bt_s, kt_s, v_s, bg_s, kg_s, gl_s, y_s, bon_s, g_s,
                 *, nb, tt):
    L = RWKV_CHUNK
    npair = MIX // LANES
    _first_step_zero(st_ref, prev_ref)

    u = u_ref[...].reshape(nb * tt, D_MODEL)
    def project(c0, c1):
        f = jnp.dot(u, w_ref[:, c0:c1], preferred_element_type=F32)
        prev = jnp.concatenate(
            [_shift_rows(f[b * tt:(b + 1) * tt], prev_ref[b, :, c0:c1], 1) for b in range(nb)], axis=0)
        for b in range(nb):
            prev_ref[b, :, c0:c1] = f[(b + 1) * tt - CARRY_ROWS:(b + 1) * tt, :]
        return f + mu_ref[:, c0:c1] * (prev - f)

    fs = project(1536, RWKV_IN)
    fa = project(0, 1536)
    lo = fs[:, :LANES]
    lane = lax.broadcasted_iota(jnp.int32, (1, LANES), 1)
    pre = _mm(jnp.where(lane < 64, jnp.tanh(lo), lo), lora_ref[...])
    lw = -RWKV_DECAY_SCALE * jax.nn.sigmoid(w0_ref[...] + pre[:, :512])
    a = jax.nn.sigmoid(a0_ref[...] + pre[:, 512:])
    g_s[...] = _mm(jax.nn.sigmoid(fs[:, LANES:]), gup_ref[...])
    r, k, v = fa[:, :512], fa[:, 512:1024], fa[:, 1024:1536]
    bd = bd_ref[...]

    def head_sum(x):
        half = bd.shape[0]
        return jnp.concatenate([_sel_rhs(x[:, i:i + half], bd, 1) for i in range(0, MIX, half)], axis=1)

    kk = k * kk_ref[...]
    kk = kk * lax.rsqrt(jnp.maximum(head_sum(kk * kk), 1e-24))
    k = k * (1.0 + (a - 1.0) * ka_ref[...])
    bon_s[...] = head_sum(r * k * rk_ref[...]) * v
    b = _chunk_cumsum(sel_ref[...], lw, nb, tt)
    bl = _chunk_last(b, L)
    eb, enb, et = jnp.exp(b), jnp.exp(-b), jnp.exp(bl - b)
    beta = kk * a
    ab_s[...] = (-kk * jnp.exp(b - lw)).astype(BF16)
    rb_s[...] = (r * eb).astype(BF16)
    bt_s[...] = (beta * enb).astype(BF16)
    kt_s[...] = (k * enb).astype(BF16)
    v_s[...] = v.astype(BF16)
    bg_s[...] = (beta * et).astype(BF16)
    kg_s[...] = (k * et).astype(BF16)
    gl_s[...] = jnp.exp(bl)

    ri = lax.broadcasted_iota(jnp.int32, (2 * L, 2 * L), 0)
    ci = lax.broadcasted_iota(jnp.int32, (2 * L, 2 * L), 1)
    same = (ri // L) == (ci // L)
    strict = jnp.logical_and(same, (ri % L) > (ci % L))
    incl = jnp.logical_and(same, (ri % L) >= (ci % L))
    m0 = lane < 64
    chains = [(bi, p) for bi in range(nb) for p in range(npair)]

    def body(c, carry):
        r0 = c * L

        def ld(ref, bi, p):
            x = ref[pl.ds(bi * tt + r0, L), p * LANES:(p + 1) * LANES]
            zero = jnp.zeros_like(x)
            return jnp.concatenate([jnp.where(m0, x, zero), jnp.where(m0, zero, x)], axis=0)

        def ld2(ref_a, ref_b, bi, p):
            return jnp.concatenate([ld(ref_a, bi, p), ld(ref_b, bi, p)], axis=0)

        gm = [_mm_nt(ld2(ab_s, rb_s, bi, p), ld2(bt_s, kt_s, bi, p)) for bi, p in chains]
        a_ab = [jnp.where(strict, x[:2 * L, :2 * L], 0.0) for x in gm]
        a_ak = [jnp.where(strict, x[:2 * L, 2 * L:], 0.0).astype(BF16) for x in gm]
        a_rb = [jnp.where(incl, x[2 * L:, :2 * L], 0.0).astype(BF16) for x in gm]
        a_rk = [jnp.where(incl, x[2 * L:, 2 * L:], 0.0).astype(BF16) for x in gm]
        st = [st_ref[bi, p] for bi, p in chains]
        stb = [x.astype(BF16) for x in st]
        uu = [_mm_nt(ld(ab_s, bi, p), s) + _mm(ak, ld(v_s, bi, p))
              for (bi, p), s, ak in zip(chains, stb, a_ak)]
        pw = [x.astype(BF16) for x in a_ab]
        levels = int(np.log2(L))
        for lvl in range(levels):
            if lvl == levels - 1:
                uu = [x + _mm(p_, x) for p_, x in zip(pw, uu)]
            else:
                res = [_mm(p_, jnp.concatenate([p_, x.astype(BF16)], axis=1)) for p_, x in zip(pw, uu)]
                pw = [x[:, :2 * L].astype(BF16) for x in res]
                uu = [x + y[:, 2 * L:] for x, y in zip(uu, res)]
        uu = [x.astype(BF16) for x in uu]
        yy = [_mm_nt(ld(rb_s, bi, p), s) + _mm(rb, x) + _mm(rk, ld(v_s, bi, p))
              for (bi, p), s, rb, rk, x in zip(chains, stb, a_rb, a_rk, uu)]
        for (bi, p), y in zip(chains, yy):
            y_s[pl.ds(bi * tt + r0, L), p * LANES:(p + 1) * LANES] = y[:L] + y[L:]
        upd = [_mm_tn(jnp.concatenate([x, ld(v_s, bi, p)], axis=0), ld2(bg_s, kg_s, bi, p))
               for (bi, p), x in zip(chains, uu)]
        for (bi, p), s, d in zip(chains, st, upd):
            st_ref[bi, p] = s * gl_s[pl.ds(bi * tt + r0, 1), p * LANES:(p + 1) * LANES] + d
        return carry

    for c in range(tt // L):
        body(c, 0)

    y = y_s[...]
    yc = y - head_sum(y) * (1.0 / 64)
    var = head_sum(yc * yc) * (1.0 / 64)
    yn = yc * lax.rsqrt(var + RWKV_GN_EPS) * gng_ref[...] + gnb_ref[...]
    out_ref[...] = ((yn + bon_s[...]) * g_s[...]).astype(BF16).reshape(nb, tt, MIX)


def _gla_kernel(u_ref, w_ref, fup_ref, fb_ref, gn_ref, sel_ref, out_ref,
                st_ref, qd_s, kd_s, kg_s, v_s, bl_s, o_s, og_s, *, nb, tt):
    L = GLA_CHUNK
    nh = 4
    nc = GLA_GROUP
    _first_step_zero(st_ref)

    u = u_ref[...].reshape(nb * tt, D_MODEL)
    f_lo = jnp.dot(u, w_ref[:, 1536:1664], preferred_element_type=F32)
    fb = jnp.dot(u, w_ref[:, :1536], preferred_element_type=F32)
    la = _log_sigmoid(_mm(f_lo, fup_ref[...]) + fb_ref[...]) * (1.0 / 16.0)
    q, k = fb[:, :256] * 0.125, fb[:, 256:512]
    v_s[...] = fb[:, 512:1024].astype(BF16)
    og_s[...] = _silu(fb[:, 1024:1536])
    b = _chunk_cumsum(sel_ref[...], la, nb, tt)
    bl = _chunk_last(b, L)
    qd_s[...] = q * jnp.exp(b)
    kd_s[...] = (k * jnp.exp(-b)).astype(BF16)
    kg_s[...] = k * jnp.exp(bl - b)
    bl_s[...] = bl

    hq = lax.broadcasted_iota(jnp.int32, (1, 256), 1) // 64
    hv = lax.broadcasted_iota(jnp.int32, (1, 512), 1) // 128
    def key_mask(c):
        ri = lax.broadcasted_iota(jnp.int32, (nh * L, nh * L * (c + 1)), 0)
        ci = lax.broadcasted_iota(jnp.int32, (nh * L, nh * L * (c + 1)), 1)
        own = jnp.logical_and((ri // L) == (ci // L), (ri % L) >= (ci % L))
        return jnp.logical_or(ci >= nh * L, own)

    masks = [key_mask(c) for c in range(nc)]
    seqs = range(nb)

    def stack(x, hid):
        x = x.astype(BF16)
        zero = jnp.zeros_like(x)
        return jnp.concatenate([jnp.where(hid == h, x, zero) for h in range(nh)], axis=0)

    def body(g, carry):
        r0 = g * (nc * L)
        keys, vals, qg, kgs, gtot = [], [], [], [], []
        for bi in seqs:
            rows = [pl.ds(bi * tt + r0 + L * c, L) for c in range(nc)]
            tot = [bl_s[pl.ds(bi * tt + r0 + L * c, 1), :] for c in range(nc)]

            def span(lo, hi):
                acc = None
                for c in range(lo, hi):
                    acc = tot[c] if acc is None else acc + tot[c]
                return 1.0 if acc is None else jnp.exp(acc)

            xq = [qd_s[r, :] for r in rows]
            xkg = [kg_s[r, :] for r in rows]
            xv = [stack(v_s[r, :], hv) for r in rows]
            keys.append([jnp.concatenate([stack(kd_s[rows[c], :], hq)]
                                         + [stack(xkg[j] * span(j + 1, c), hq) for j in range(c - 1, -1, -1)], axis=0)
                         for c in range(nc)])
            vals.append([jnp.concatenate([xv[c]] + [xv[j] for j in range(c - 1, -1, -1)], axis=0) for c in range(nc)])
            qg.append([stack(xq[c], hq) for c in range(nc)]
                      + [jnp.concatenate([stack(xq[c] * span(0, c), hq) for c in range(nc)], axis=0)])
            kgs.append((jnp.concatenate(xv, axis=0),
                        jnp.concatenate([stack(xkg[c] * span(c + 1, nc), hq) for c in range(nc)], axis=0)))
            gtot.append(span(0, nc))
        att = [[jnp.where(masks[c], _mm_nt(qg[bi][c], keys[bi][c]), 0.0) for c in range(nc)] for bi in seqs]
        upd = [_mm_tn(*kgs[bi]) for bi in seqs]
        st = [st_ref[bi] for bi in seqs]
        oi = [_mm_nt(qg[bi][nc], st[bi]) for bi in seqs]
        for bi in seqs:
            for c in range(nc):
                oo = _mm(att[bi][c], vals[bi][c]) + oi[bi][c * nh * L:(c + 1) * nh * L]
                acc = oo[:L]
                for h in range(1, nh):
                    acc = acc + oo[h * L:(h + 1) * L]
                o_s[pl.ds(bi * tt + r0 + L * c, L), :] = acc
            st_ref[bi] = st[bi] * gtot[bi] + upd[bi]
        return carry

    for g in range(tt // (nc * L)):
        body(g, 0)

    for h in range(nh):
        cs = slice(h * LANES, (h + 1) * LANES)
        oh = o_s[:, cs]
        oh = oh * lax.rsqrt(jnp.mean(oh * oh, axis=-1, keepdims=True) + EPS) * gn_ref[...]
        out_ref[:, :, cs] = (oh * og_s[:, cs]).astype(BF16).reshape(nb, tt, LANES)


def _mlstm_kernel(u_ref, w_ref, cw_ref, cb_ref, ib_ref, fbias_ref, gn_ref, sel_ref, eli_ref, eb_ref,
                  out_ref, c_ref, m_ref, prev_ref, q_s, k_s, v_s, bf_s, lf_s, h_s, og_s, *, nb, tt):
    L = MLSTM_CHUNK
    nh = 4
    _first_step_zero(c_ref, m_ref, prev_ref)

    u = u_ref[...].reshape(nb * tt, D_MODEL)
    misc = jnp.dot(u, w_ref[:, 1536:1664], preferred_element_type=F32)
    fc = jnp.dot(u, w_ref[:, :1536], preferred_element_type=F32)
    qk = fc[:, :512]
    conv = _causal_conv(qk, [prev_ref[b] for b in range(nb)], cw_ref[...], cb_ref[...], tt)
    _save_carry(prev_ref, qk, nb, tt)
    qk = _silu(conv)
    q_s[...] = qk[:, :256].astype(BF16)
    k_s[...] = qk[:, 256:] * 0.125
    v_s[...] = fc[:, 512:1024].astype(BF16)
    og_s[...] = jax.nn.sigmoid(fc[:, 1024:1536])
    lane = lax.broadcasted_iota(jnp.int32, (1, LANES), 1)
    li = misc + ib_ref[...]
    b = _chunk_cumsum(sel_ref[...], _log_sigmoid(misc + fbias_ref[...]), nb, tt)
    lf_s[...] = _sel_rhs(li, eli_ref[...], 2)
    bf_s[...] = _sel_rhs(b, eb_ref[...], 3)

    hq = lax.broadcasted_iota(jnp.int32, (1, 256), 1) // 64
    ri = lax.broadcasted_iota(jnp.int32, (nh * L, nh * L), 0)
    ci = lax.broadcasted_iota(jnp.int32, (nh * L, nh * L), 1)
    dmask = jnp.logical_and((ri // L) == (ci // L), (ri % L) >= (ci % L))
    ones = jnp.ones((L, LANES), BF16)
    seqs = range(nb)

    def stackq(x):
        zero = jnp.zeros_like(x)
        return jnp.concatenate([jnp.where(hq == h, x, zero) for h in range(nh)], axis=0)

    def headcol(ref, rs):
        x = ref[rs, :]
        return jnp.concatenate([x[:, h * LANES:(h + 1) * LANES] for h in range(nh)], axis=0)

    def perhead_last(col):
        return jnp.concatenate(
            [jnp.broadcast_to(col[(h + 1) * L - 1:(h + 1) * L, :], (L, LANES)) for h in range(nh)], axis=0)

    def wide(col):
        return jnp.concatenate([col, col], axis=1)

    def body(c, carry):
        r0 = c * L
        rows = [pl.ds(bi * tt + r0, L) for bi in seqs]
        xq = [stackq(q_s[rs, :]) for rs in rows]
        kf = [stackq(k_s[rs, :]) for rs in rows]
        xv = []
        for rs in rows:
            vc = v_s[rs, :]
            xv.append(jnp.concatenate(
                [jnp.concatenate([vc[:, h * LANES:(h + 1) * LANES], ones], axis=1) for h in range(nh)], axis=0))
        qk = [_mm_nt(xq[bi], kf[bi]) for bi in seqs]
        qc = [_mm(xq[bi], c_ref[bi]) for bi in seqs]
        sc, m_ts, inters, w_lasts = [], [], [], []
        for bi in seqs:
            rs = rows[bi]
            bcol, licol = headcol(bf_s, rs), headcol(lf_s, rs)
            mcol = m_ref[bi]
            zt = (licol - bcol).T
            log_d = jnp.where(dmask, wide(bcol) + jnp.concatenate([zt, zt], axis=0), -jnp.inf)
            mx = jnp.broadcast_to(jnp.max(log_d, axis=-1, keepdims=True), (nh * L, LANES))
            m_t = jnp.maximum(bcol + mcol, mx)
            sc.append((qk[bi] * jnp.exp(log_d - wide(m_t))).astype(BF16))
            m_ts.append(m_t)
            inters.append(jnp.exp(bcol + mcol - m_t))
            w_lasts.append(jnp.exp(perhead_last(bcol) - bcol + licol - perhead_last(m_t)))
        num = [_mm(sc[bi], xv[bi]) for bi in seqs]
        upd = [_mm_tn(kf[bi] * wide(w_lasts[bi]), xv[bi]) for bi in seqs]
        for bi in seqs:
            tot = num[bi] + qc[bi] * wide(inters[bi])
            hh = tot[:, :LANES] / jnp.maximum(jnp.abs(tot[:, LANES:]), jnp.exp(-m_ts[bi]))
            for h in range(nh):
                h_s[rows[bi], h * LANES:(h + 1) * LANES] = hh[h * L:(h + 1) * L, :]
            c_ref[bi] = wide(perhead_last(inters[bi])) * c_ref[bi] + upd[bi]
            m_ref[bi] = perhead_last(m_ts[bi])
        return carry

    for c in range(tt // L):
        body(c, 0)

    for h in range(nh):
        cs = slice(h * LANES, (h + 1) * LANES)
        hh = h_s[:, cs]
        hh = hh * lax.rsqrt(jnp.mean(hh * hh, axis=-1, keepdims=True) + EPS) * gn_ref[:, cs]
        out_ref[:, :, cs] = (hh * og_s[:, cs]).astype(BF16).reshape(nb, tt, LANES)


def _ssd_kernel(u_ref, w_ref, cw_ref, cb_ref, dtb_ref, alog_ref, dsk_ref, gn_ref, sel_ref,
                ex_ref, out_ref, st_ref, prev_ref, *, nb, tt):
    L = SSD_CHUNK
    assert tt == L
    _first_step_zero(st_ref, prev_ref)

    u = u_ref[...].reshape(nb * tt, D_MODEL)
    dt_pre = jnp.dot(u, w_ref[:, 1536:1664], preferred_element_type=F32)
    fd = jnp.dot(u, w_ref[:, :1536], preferred_element_type=F32)
    z = _silu(fd[:, :512])
    xbc = fd[:, 512:1536]
    conv = _silu(_causal_conv(xbc, [prev_ref[b] for b in range(nb)], cw_ref[...], cb_ref[...], tt))
    _save_carry(prev_ref, xbc, nb, tt)
    x = conv[:, :512]
    bm = conv[:, 512:768].astype(BF16)
    cm = conv[:, 768:1024].astype(BF16)
    lane = lax.broadcasted_iota(jnp.int32, (1, LANES), 1)
    valid = lane < 8
    dt = jnp.where(valid, _softplus(dt_pre + dtb_ref[...]), 0.0)
    dta = dt * jnp.where(valid, -jnp.exp(alog_ref[...]), 0.0)
    ac = _chunk_cumsum(sel_ref[...], dta, nb, tt, 3)
    ace = _sel_rhs(ac, ex_ref[...], 3)
    xd = x * _sel_rhs(dt, ex_ref[...], 2)

    ri = lax.broadcasted_iota(jnp.int32, (L, L), 0)
    ci = lax.broadcasted_iota(jnp.int32, (L, L), 1)
    tril = ri >= ci
    m0 = lane < 64
    npair = MIX // LANES
    seqs = range(nb)
    seg = lambda a, bi: a[bi * tt:(bi + 1) * tt]
    groups = [(bi, g) for bi in seqs for g in range(2)]
    chains = [(bi, p) for bi in seqs for p in range(npair)]

    cb = {(bi, g): _mm_nt(seg(cm, bi)[:, g * LANES:(g + 1) * LANES], seg(bm, bi)[:, g * LANES:(g + 1) * LANES])
          for bi, g in groups}
    last = {bi: seg(ace, bi)[L - 1:L, :] for bi in seqs}
    ys = {}
    for bi, p in chains:
        cs = slice(p * LANES, (p + 1) * LANES)
        gs = slice((p // 2) * LANES, (p // 2 + 1) * LANES)
        ys[bi, p] = _mm(seg(cm, bi)[:, gs], st_ref[bi, p]) * jnp.exp(seg(ace, bi)[:, cs])
    for bi, p in chains:
        cs = slice(p * LANES, (p + 1) * LANES)
        gs = slice((p // 2) * LANES, (p // 2 + 1) * LANES)
        xdec = seg(xd, bi)[:, cs] * jnp.exp(last[bi][:, cs] - seg(ace, bi)[:, cs])
        st_ref[bi, p] = st_ref[bi, p] * jnp.exp(last[bi][:, cs]) + _mm_tn(seg(bm, bi)[:, gs], xdec)
    for bi in seqs:
        acc = seg(ac, bi)
        act = acc.T
        for p in range(npair):
            cs = slice(p * LANES, (p + 1) * LANES)
            xp = seg(xd, bi)[:, cs]
            y = ys[bi, p]
            for hh, xm in ((2 * p, jnp.where(m0, xp, 0.0)), (2 * p + 1, jnp.where(m0, 0.0, xp))):
                sg = jnp.exp(jnp.where(tril, acc[:, hh:hh + 1] - act[hh:hh + 1, :], -jnp.inf))
                y = y + _mm(cb[bi, p // 2] * sg, xm)
            ys[bi, p] = y
    y = jnp.concatenate([jnp.concatenate([ys[bi, p] for p in range(npair)], axis=1) for bi in seqs], axis=0)
    y = (y + x * dsk_ref[...]) * z
    out_ref[...] = _rms(y, gn_ref[...]).astype(BF16).reshape(nb, tt, MIX)


def _merge_kernel(h_ref, u_ref, wg_ref, b0_ref, b1_ref, b2_ref, b3_ref, p_ref, wo_ref, gnext_ref,
                  out_ref, unext_ref):
    u = u_ref[...]
    acc = None
    for i, br in enumerate((b0_ref, b1_ref, b2_ref, b3_ref)):
        gate = jax.nn.sigmoid(jnp.dot(u, wg_ref[:, i * D_MODEL:(i + 1) * D_MODEL], preferred_element_type=F32))
        term = gate * jnp.dot(br[...], p_ref[i], preferred_element_type=F32)
        acc = term if acc is None else acc + term
    h = h_ref[...] + _mm(acc, wo_ref[...])
    out_ref[...] = h
    unext_ref[...] = _rms(h, gnext_ref[...]).astype(BF16)


def _ffn_kernel(h_ref, u_ref, up_ref, cw_ref, cb_ref, dn_ref, gnext_ref, *rest, tm, tiles_per_seq, tf, last):
    out_refs, (a_s, carry_s) = rest[:-2], rest[-2:]
    u = u_ref[...]
    seq_start = pl.program_id(0) % tiles_per_seq == 0

    def half(off):
        cs = slice(off, off + tf)
        z = jnp.dot(u, up_ref[:, cs], preferred_element_type=F32)
        carry = jnp.where(seq_start, 0.0, carry_s[:, cs])
        carry_s[:, cs] = z[tm - CARRY_ROWS:, :]
        return _causal_conv(z, [carry], cw_ref[:, cs], cb_ref[:, cs], tm)

    for j in range(D_FF // tf):
        gate, val = half(j * tf), half(D_FF + j * tf)
        a_s[:, j * tf:(j + 1) * tf] = (_silu(gate) * val).astype(BF16)
    h = h_ref[...] + jnp.dot(a_s[...], dn_ref[...], preferred_element_type=F32)
    if last:
        out_refs[0][...] = _rms(h, gnext_ref[...])
    else:
        out_refs[0][...] = h
        out_refs[1][...] = _rms(h, gnext_ref[...]).astype(BF16)


def _norm_kernel(h_ref, g_ref, out_ref):
    out_ref[...] = _rms(h_ref[...], g_ref[...]).astype(out_ref.dtype)


def _layer_spec(arr, l):
    tail = arr.shape[1:]
    return pl.BlockSpec((None,) + tail, lambda *_, l=l, n=len(tail): (l,) + (0,) * n)


def _full_spec(arr):
    return pl.BlockSpec(arr.shape, lambda *_, n=arr.ndim: (0,) * n)


def _mixer_call(kern, name, h, l, layer_params, consts, scratch, tt):
    bsz, t, d = h.shape
    in_specs = [pl.BlockSpec((bsz, tt, d), lambda i: (0, i, 0))]
    in_specs += [_layer_spec(a, l) for a in layer_params]
    in_specs += [_full_spec(a) for a in consts]
    return pl.pallas_call(
        functools.partial(kern, nb=bsz, tt=tt),
        grid=(t // tt,),
        in_specs=in_specs,
        out_specs=pl.BlockSpec((bsz, tt, MIX), lambda i: (0, i, 0)),
        out_shape=jax.ShapeDtypeStruct((bsz, t, MIX), BF16),
        scratch_shapes=scratch,
        compiler_params=pltpu.CompilerParams(dimension_semantics=("arbitrary",),
                                             vmem_limit_bytes=VMEM_LIMIT),
        name=name,
    )(h, *layer_params, *consts)


def _vm(shape, dtype=F32):
    return pltpu.VMEM(shape, dtype)


def _block_diag_ones(n, blk, lower=False):
    i = np.arange(n)
    m = (i[:, None] // blk) == (i[None, :] // blk)
    if lower:
        m = m & (i[:, None] >= i[None, :])
    return m


def _chunk_selector(tt, chunk):
    return jnp.asarray(_block_diag_ones(tt, chunk, lower=True), BF16)


def _pad_lanes(a, width=LANES):
    return jnp.pad(a, [(0, 0)] * (a.ndim - 1) + [(0, width - a.shape[-1])])


def _row(a):
    return a[:, None, :]


def kernel(x, mix_norm_g, w_in, rwkv_mu, rwkv_w0, rwkv_w_up, rwkv_a0, rwkv_a_up, rwkv_g_up, rwkv_k_k, rwkv_k_a, rwkv_r_k, rwkv_gn_g, rwkv_gn_b, gla_f_up, gla_f_bias, gla_norm_g, mlstm_conv_w, mlstm_conv_b, mlstm_i_bias, mlstm_f_bias, mlstm_norm_g, ssd_conv_w, ssd_conv_b, ssd_dt_bias, ssd_a_log, ssd_d, ssd_norm_g, branch_proj, w_out, ffn_norm_g, ffn_up, ffn_conv_w, ffn_conv_b, ffn_down, final_norm_g):
    bsz, t, d = x.shape
    depth = w_in.shape[0]
    tt = MIXER_TILE
    rt = bsz * tt
    assert d == D_MODEL and t % tt == 0

    def cols(lo, hi):
        return w_in[:, :, lo:hi]

    zpad = lambda n: jnp.zeros((depth, d, n), F32)
    w_rwkv = cols(0, RWKV_IN).astype(BF16)
    o = OFF_GLA
    w_gla = jnp.concatenate([cols(o, o + 1024), cols(o + 1040, o + 1552), cols(o + 1024, o + 1040),
                             zpad(LANES - 16)], axis=-1).astype(BF16)
    o = OFF_MLSTM
    w_mlstm = jnp.concatenate([cols(o, o + 1024), cols(o + 1032, o + 1544), cols(o + 1024, o + 1032),
                               zpad(LANES - 8)], axis=-1).astype(BF16)
    o = OFF_SSD
    w_ssd = jnp.concatenate([cols(o, o + 1544), zpad(LANES - 8)], axis=-1).astype(BF16)
    w_gate = cols(OFF_GATE, OFF_GATE + N_BRANCH * d).astype(BF16)

    ng = _row(mix_norm_g)
    zl = jnp.zeros_like(rwkv_w_up)
    rwkv_lora = jnp.concatenate([jnp.concatenate([rwkv_w_up, zl], axis=-1),
                                 jnp.concatenate([zl, rwkv_a_up], axis=-1)], axis=1).astype(BF16)
    rwkv_params = [w_rwkv, _row(rwkv_mu), _row(rwkv_w0), _row(rwkv_a0), rwkv_lora,
                   rwkv_g_up.astype(BF16), _row(rwkv_k_k), _row(rwkv_k_a),
                   _row(rwkv_r_k.reshape(depth, MIX)), _row(rwkv_gn_g), _row(rwkv_gn_b)]
    rwkv_consts = [jnp.asarray(_block_diag_ones(2 * LANES, 64), BF16), _chunk_selector(tt, RWKV_CHUNK)]
    rwkv_scratch = ([_vm((bsz, MIX // LANES, LANES, LANES)), _vm((bsz, CARRY_ROWS, RWKV_IN))]
                    + [_vm((rt, MIX), BF16)] * 7 + [_vm((rt, MIX))] * 4)

    gla_fup = jnp.pad(gla_f_up, ((0, 0), (0, LANES - gla_f_up.shape[1]), (0, 0))).astype(BF16)
    gla_params = [w_gla, gla_fup, _row(gla_f_bias), _row(gla_norm_g)]
    gla_consts = [_chunk_selector(tt, GLA_CHUNK)]
    gla_scratch = [_vm((bsz, MIX, 256)), _vm((rt, 256)), _vm((rt, 256), BF16), _vm((rt, 256)),
                   _vm((rt, MIX), BF16), _vm((rt, 256)), _vm((rt, MIX)), _vm((rt, MIX))]

    zh = jnp.zeros_like(mlstm_i_bias)
    mlstm_params = [w_mlstm, mlstm_conv_w, _row(mlstm_conv_b),
                    _row(_pad_lanes(jnp.concatenate([mlstm_i_bias, zh], axis=-1))),
                    _row(_pad_lanes(jnp.concatenate([zh, mlstm_f_bias], axis=-1))),
                    _row(mlstm_norm_g)]
    tile_of_lane = np.arange(MIX) // LANES
    lane_to_tile = lambda off: jnp.asarray(np.arange(LANES)[:, None] == off + tile_of_lane[None, :], BF16)
    mlstm_consts = [_chunk_selector(tt, MLSTM_CHUNK), lane_to_tile(0), lane_to_tile(4)]
    mlstm_scratch = [_vm((bsz, 256, 256)), _vm((bsz, 256, LANES)), _vm((bsz, CARRY_ROWS, 512)),
                     _vm((rt, 256), BF16), _vm((rt, 256)), _vm((rt, MIX), BF16), _vm((rt, MIX)),
                     _vm((rt, MIX)), _vm((rt, MIX)), _vm((rt, MIX))]

    ssd_params = [w_ssd, ssd_conv_w, _row(ssd_conv_b), _row(_pad_lanes(ssd_dt_bias)),
                  _row(_pad_lanes(ssd_a_log)), _row(jnp.repeat(ssd_d, 64, axis=-1)), _row(ssd_norm_g)]
    hid = np.arange(MIX) // 64
    expand = jnp.asarray(np.arange(LANES)[:, None] == hid[None, :], BF16)
    ssd_consts = [_chunk_selector(tt, SSD_CHUNK), expand]
    ssd_scratch = [_vm((bsz, MIX // LANES, LANES, LANES)), _vm((bsz, CARRY_ROWS, 1024))]

    bproj = branch_proj.astype(BF16)
    wout = w_out.astype(BF16)
    fng = _row(ffn_norm_g)
    fup = ffn_up.astype(BF16)
    fdn = ffn_down.astype(BF16)
    fcb = _row(ffn_conv_b)

    m = bsz * t
    tm = 512
    tf = 2 * LANES
    assert t % tm == 0 and D_FF % tf == 0
    cparams = pltpu.CompilerParams(dimension_semantics=("arbitrary",), vmem_limit_bytes=VMEM_LIMIT)
    row_spec = pl.BlockSpec((tm, d), lambda i: (i, 0))
    br_spec = pl.BlockSpec((tm, MIX), lambda i: (i, 0))

    def resident(arr, l):
        tail = arr.shape[1:]
        return pl.BlockSpec((None,) + tail, lambda i, l=l, n=len(tail): (l,) + (0,) * n,
                            pipeline_mode=pl.Buffered(1))

    def merge(h2, u2, l, branches):
        return pl.pallas_call(
            _merge_kernel,
            grid=(m // tm,),
            in_specs=[row_spec, row_spec, resident(w_gate, l), br_spec, br_spec, br_spec, br_spec,
                      resident(bproj, l), resident(wout, l), resident(fng, l)],
            out_specs=[row_spec, row_spec],
            out_shape=[jax.ShapeDtypeStruct((m, d), F32), jax.ShapeDtypeStruct((m, d), BF16)],
            compiler_params=cparams,
            name="merge",
        )(h2, u2, w_gate, *branches, bproj, wout, fng)

    def ffn(h2, u2, l):
        last = l == depth - 1
        if last:
            gnext, gspec = final_norm_g[None, :], pl.BlockSpec((1, d), lambda i: (0, 0))
            out_specs, out_shape = row_spec, jax.ShapeDtypeStruct((m, d), x.dtype)
        else:
            gnext, gspec = ng, resident(ng, l + 1)
            out_specs = [row_spec, row_spec]
            out_shape = [jax.ShapeDtypeStruct((m, d), F32), jax.ShapeDtypeStruct((m, d), BF16)]
        return pl.pallas_call(
            functools.partial(_ffn_kernel, tm=tm, tiles_per_seq=t // tm, tf=tf, last=last),
            grid=(m // tm,),
            in_specs=[row_spec, row_spec, resident(fup, l), resident(ffn_conv_w, l), resident(fcb, l),
                      resident(fdn, l), gspec],
            out_specs=out_specs,
            out_shape=out_shape,
            scratch_shapes=[_vm((tm, D_FF), BF16), _vm((CARRY_ROWS, 2 * D_FF))],
            compiler_params=cparams,
            name="ffn",
        )(h2, u2, fup, ffn_conv_w, fcb, fdn, gnext)

    h = x.reshape(m, d)
    u = pl.pallas_call(
        _norm_kernel,
        grid=(m // tm,),
        in_specs=[row_spec, resident(ng, 0)],
        out_specs=row_spec,
        out_shape=jax.ShapeDtypeStruct((m, d), BF16),
        name="first_norm",
    )(h, ng)
    for l in range(depth):
        u3 = u.reshape(bsz, t, d)
        branches = [
            _mixer_call(_rwkv_kernel, "rwkv", u3, l, rwkv_params, rwkv_consts, rwkv_scratch, tt),
            _mixer_call(_gla_kernel, "gla", u3, l, gla_params, gla_consts, gla_scratch, tt),
            _mixer_call(_mlstm_kernel, "mlstm", u3, l, mlstm_params, mlstm_consts, mlstm_scratch, tt),
            _mixer_call(_ssd_kernel, "ssd", u3, l, ssd_params, ssd_consts, ssd_scratch, tt),
        ]
        h, u = merge(h, u, l, [b.reshape(m, MIX) for b in branches])
        if l == depth - 1:
            h = ffn(h, u, l)
        else:
            h, u = ffn(h, u, l)
    return h.reshape(bsz, t, d)
```

```python
import functools

import numpy as np
import jax
import jax.numpy as jnp
from jax import lax
from jax.experimental import pallas as pl
from jax.experimental.pallas import tpu as pltpu

F32, BF16 = jnp.float32, jnp.bfloat16

D_MODEL = 1024
MIX = 512
EPS = 1e-6
N_BRANCH = 4
RWKV_DECAY_SCALE = 0.6065306597
RWKV_GN_EPS = 64e-5
RWKV_CHUNK = 64
GLA_CHUNK = 16
GLA_GROUP = 4
MLSTM_CHUNK = 64
SSD_CHUNK = 128
D_FF = 2816

RWKV_IN, GLA_IN, MLSTM_IN, SSD_IN = 1792, 1552, 1544, 1544
OFF_GLA = RWKV_IN
OFF_MLSTM = OFF_GLA + GLA_IN
OFF_SSD = OFF_MLSTM + MLSTM_IN
OFF_GATE = OFF_SSD + SSD_IN

LANES = 128
CARRY_ROWS = 8
MIXER_TILE = 128
VMEM_LIMIT = 48 * 1024 * 1024


def _mm(a, b):
    return jnp.dot(a.astype(BF16), b.astype(BF16), preferred_element_type=F32)


def _mm_nt(a, b):
    return lax.dot_general(a.astype(BF16), b.astype(BF16), (((1,), (1,)), ((), ())),
                           preferred_element_type=F32)


def _mm_tn(a, b):
    return lax.dot_general(a.astype(BF16), b.astype(BF16), (((0,), (0,)), ((), ())),
                           preferred_element_type=F32)


def _bf16_terms(x, n):
    terms, rest = [], x
    for _ in range(n):
        t = rest.astype(BF16)
        terms.append(t)
        rest = rest - t.astype(F32)
    return terms


def _sel_lhs(sel, x, n=3):
    out = None
    for t in _bf16_terms(x, n):
        p = jnp.dot(sel, t, preferred_element_type=F32)
        out = p if out is None else out + p
    return out


def _sel_rhs(x, sel, n=2):
    out = None
    for t in _bf16_terms(x, n):
        p = jnp.dot(t, sel, preferred_element_type=F32)
        out = p if out is None else out + p
    return out


def _chunk_cumsum(sel, x, nb, tt, n=2):
    return jnp.concatenate([_sel_lhs(sel, x[b * tt:(b + 1) * tt], n) for b in range(nb)], axis=0)


def _chunk_last(b, chunk):
    return jnp.concatenate(
        [jnp.broadcast_to(b[i + chunk - 1:i + chunk, :], (chunk, b.shape[1])) for i in range(0, b.shape[0], chunk)],
        axis=0)


def _rms(x, g):
    return x * lax.rsqrt(jnp.mean(x * x, axis=-1, keepdims=True) + EPS) * g


def _silu(x):
    return x * jax.nn.sigmoid(x)


def _softplus(x):
    return jnp.maximum(x, 0.0) + jnp.log1p(jnp.exp(-jnp.abs(x)))


def _log_sigmoid(x):
    return -_softplus(-x)


def _shift_rows(x, carry, s):
    y = pltpu.roll(x, s, 0)
    rows = lax.broadcasted_iota(jnp.int32, (CARRY_ROWS, 1), 0)
    head = y[:CARRY_ROWS]
    for q in range(s):
        src = CARRY_ROWS - s + q
        head = jnp.where(rows == q, carry[src:src + 1, :], head)
    return jnp.concatenate([head, y[CARRY_ROWS:]], axis=0)


def _causal_conv(x, carries, w, b, tt):
    k = w.shape[0]
    y = w[k - 1:k, :] * x + b
    for s in range(1, k):
        sh = jnp.concatenate([_shift_rows(x[i * tt:(i + 1) * tt], c, s) for i, c in enumerate(carries)],
                             axis=0)
        y = y + w[k - 1 - s:k - s, :] * sh
    return y


def _save_carry(carry_ref, x, nb, tt):
    for b in range(nb):
        carry_ref[b] = x[(b + 1) * tt - CARRY_ROWS:(b + 1) * tt, :]


def _first_step_zero(*refs):
    @pl.when(pl.program_id(0) == 0)
    def _init():
        for r in refs:
            def body(i, carry, r=r):
                r[i] = jnp.zeros(r.shape[1:], r.dtype)
                return carry

            lax.fori_loop(0, r.shape[0], body, 0)


def _rwkv_kernel(u_ref, w_ref, mu_ref, w0_ref, a0_ref, lora_ref, gup_ref, kk_ref, ka_ref,
                 rk_ref, gng_ref, gnb_ref, bd_ref, sel_ref, out_ref,
                 st_ref, prev_ref, ab_s, rb_s, bt_s, kt_s, v_s, bg_s, kg_s, gl_s, y_s, bon_s, g_s,
                 *, nb, tt):
    L = RWKV_CHUNK
    npair = MIX // LANES
    _first_step_zero(st_ref, prev_ref)

    u = u_ref[...].reshape(nb * tt, D_MODEL)
    def project(c0, c1):
        f = jnp.dot(u, w_ref[:, c0:c1], preferred_element_type=F32)
        prev = jnp.concatenate(
            [_shift_rows(f[b * tt:(b + 1) * tt], prev_ref[b, :, c0:c1], 1) for b in range(nb)], axis=0)
        for b in range(nb):
            prev_ref[b, :, c0:c1] = f[(b + 1) * tt - CARRY_ROWS:(b + 1) * tt, :]
        return f + mu_ref[:, c0:c1] * (prev - f)

    fs = project(1536, RWKV_IN)
    fa = project(0, 1536)
    lo = fs[:, :LANES]
    lane = lax.broadcasted_iota(jnp.int32, (1, LANES), 1)
    pre = _mm(jnp.where(lane < 64, jnp.tanh(lo), lo), lora_ref[...])
    lw = -RWKV_DECAY_SCALE * jax.nn.sigmoid(w0_ref[...] + pre[:, :512])
    a = jax.nn.sigmoid(a0_ref[...] + pre[:, 512:])
    g_s[...] = _mm(jax.nn.sigmoid(fs[:, LANES:]), gup_ref[...])
    r, k, v = fa[:, :512], fa[:, 512:1024], fa[:, 1024:1536]
    bd = bd_ref[...]

    def head_sum(x):
        half = bd.shape[0]
        return jnp.concatenate([_sel_rhs(x[:, i:i + half], bd, 1) for i in range(0, MIX, half)], axis=1)

    kk = k * kk_ref[...]
    kk = kk * lax.rsqrt(jnp.maximum(head_sum(kk * kk), 1e-24))
    k = k * (1.0 + (a - 1.0) * ka_ref[...])
    bon_s[...] = head_sum(r * k * rk_ref[...]) * v
    b = _chunk_cumsum(sel_ref[...], lw, nb, tt)
    bl = _chunk_last(b, L)
    eb, enb, et = jnp.exp(b), jnp.exp(-b), jnp.exp(bl - b)
    beta = kk * a
    ab_s[...] = (-kk * jnp.exp(b - lw)).astype(BF16)
    rb_s[...] = (r * eb).astype(BF16)
    bt_s[...] = (beta * enb).astype(BF16)
    kt_s[...] = (k * enb).astype(BF16)
    v_s[...] = v.astype(BF16)
    bg_s[...] = (beta * et).astype(BF16)
    kg_s[...] = (k * et).astype(BF16)
    gl_s[...] = jnp.exp(bl)

    ri = lax.broadcasted_iota(jnp.int32, (2 * L, 2 * L), 0)
    ci = lax.broadcasted_iota(jnp.int32, (2 * L, 2 * L), 1)
    same = (ri // L) == (ci // L)
    strict = jnp.logical_and(same, (ri % L) > (ci % L))
    incl = jnp.logical_and(same, (ri % L) >= (ci % L))
    m0 = lane < 64
    chains = [(bi, p) for bi in range(nb) for p in range(npair)]

    def body(c, carry):
        r0 = c * L

        def ld(ref, bi, p):
            x = ref[pl.ds(bi * tt + r0, L), p * LANES:(p + 1) * LANES]
            zero = jnp.zeros_like(x)
            return jnp.concatenate([jnp.where(m0, x, zero), jnp.where(m0, zero, x)], axis=0)

        def ld2(ref_a, ref_b, bi, p):
            return jnp.concatenate([ld(ref_a, bi, p), ld(ref_b, bi, p)], axis=0)

        gm = [_mm_nt(ld2(ab_s, rb_s, bi, p), ld2(bt_s, kt_s, bi, p)) for bi, p in chains]
        a_ab = [jnp.where(strict, x[:2 * L, :2 * L], 0.0) for x in gm]
        a_ak = [jnp.where(strict, x[:2 * L, 2 * L:], 0.0).astype(BF16) for x in gm]
        a_rb = [jnp.where(incl, x[2 * L:, :2 * L], 0.0).astype(BF16) for x in gm]
        a_rk = [jnp.where(incl, x[2 * L:, 2 * L:], 0.0).astype(BF16) for x in gm]
        st = [st_ref[bi, p] for bi, p in chains]
        stb = [x.astype(BF16) for x in st]
        uu = [_mm_nt(ld(ab_s, bi, p), s) + _mm(ak, ld(v_s, bi, p))
              for (bi, p), s, ak in zip(chains, stb, a_ak)]
        pw = [x.astype(BF16) for x in a_ab]
        levels = int(np.log2(L))
        for lvl in range(levels):
            if lvl == levels - 1:
                uu = [x + _mm(p_, x) for p_, x in zip(pw, uu)]
            else:
                res = [_mm(p_, jnp.concatenate([p_, x.astype(BF16)], axis=1)) for p_, x in zip(pw, uu)]
                pw = [x[:, :2 * L].astype(BF16) for x in res]
                uu = [x + y[:, 2 * L:] for x, y in zip(uu, res)]
        uu = [x.astype(BF16) for x in uu]
        yy = [_mm_nt(ld(rb_s, bi, p), s) + _mm(rb, x) + _mm(rk, ld(v_s, bi, p))
              for (bi, p), s, rb, rk, x in zip(chains, stb, a_rb, a_rk, uu)]
        for (bi, p), y in zip(chains, yy):
            y_s[pl.ds(bi * tt + r0, L), p * LANES:(p + 1) * LANES] = y[:L] + y[L:]
        upd = [_mm_tn(jnp.concatenate([x, ld(v_s, bi, p)], axis=0), ld2(bg_s, kg_s, bi, p))
               for (bi, p), x in zip(chains, uu)]
        for (bi, p), s, d in zip(chains, st, upd):
            st_ref[bi, p] = s * gl_s[pl.ds(bi * tt + r0, 1), p * LANES:(p + 1) * LANES] + d
        return carry

    for c in range(tt // L):
        body(c, 0)

    y = y_s[...]
    yc = y - head_sum(y) * (1.0 / 64)
    var = head_sum(yc * yc) * (1.0 / 64)
    yn = yc * lax.rsqrt(var + RWKV_GN_EPS) * gng_ref[...] + gnb_ref[...]
    out_ref[...] = ((yn + bon_s[...]) * g_s[...]).astype(BF16).reshape(nb, tt, MIX)


def _gla_kernel(u_ref, w_ref, fup_ref, fb_ref, gn_ref, sel_ref, out_ref,
                st_ref, qd_s, kd_s, kg_s, v_s, bl_s, o_s, og_s, *, nb, tt):
    L = GLA_CHUNK
    nh = 4
    nc = GLA_GROUP
    _first_step_zero(st_ref)

    u = u_ref[...].reshape(nb * tt, D_MODEL)
    f_lo = jnp.dot(u, w_ref[:, 1536:1664], preferred_element_type=F32)
    fb = jnp.dot(u, w_ref[:, :1536], preferred_element_type=F32)
    la = _log_sigmoid(_mm(f_lo, fup_ref[...]) + fb_ref[...]) * (1.0 / 16.0)
    q, k = fb[:, :256] * 0.125, fb[:, 256:512]
    v_s[...] = fb[:, 512:1024].astype(BF16)
    og_s[...] = _silu(fb[:, 1024:1536])
    b = _chunk_cumsum(sel_ref[...], la, nb, tt)
    bl = _chunk_last(b, L)
    qd_s[...] = q * jnp.exp(b)
    kd_s[...] = (k * jnp.exp(-b)).astype(BF16)
    kg_s[...] = k * jnp.exp(bl - b)
    bl_s[...] = bl

    hq = lax.broadcasted_iota(jnp.int32, (1, 256), 1) // 64
    hv = lax.broadcasted_iota(jnp.int32, (1, 512), 1) // 128
    def key_mask(c):
        ri = lax.broadcasted_iota(jnp.int32, (nh * L, nh * L * (c + 1)), 0)
        ci = lax.broadcasted_iota(jnp.int32, (nh * L, nh * L * (c + 1)), 1)
        own = jnp.logical_and((ri // L) == (ci // L), (ri % L) >= (ci % L))
        return jnp.logical_or(ci >= nh * L, own)

    masks = [key_mask(c) for c in range(nc)]
    seqs = range(nb)

    def stack(x, hid):
        x = x.astype(BF16)
        zero = jnp.zeros_like(x)
        return jnp.concatenate([jnp.where(hid == h, x, zero) for h in range(nh)], axis=0)

    def body(g, carry):
        r0 = g * (nc * L)
        keys, vals, qg, kgs, gtot = [], [], [], [], []
        for bi in seqs:
            rows = [pl.ds(bi * tt + r0 + L * c, L) for c in range(nc)]
            tot = [bl_s[pl.ds(bi * tt + r0 + L * c, 1), :] for c in range(nc)]

            def span(lo, hi):
                acc = None
                for c in range(lo, hi):
                    acc = tot[c] if acc is None else acc + tot[c]
                return 1.0 if acc is None else jnp.exp(acc)

            xq = [qd_s[r, :] for r in rows]
            xkg = [kg_s[r, :] for r in rows]
            xv = [stack(v_s[r, :], hv) for r in rows]
            keys.append([jnp.concatenate([stack(kd_s[rows[c], :], hq)]
                                         + [stack(xkg[j] * span(j + 1, c), hq) for j in range(c - 1, -1, -1)], axis=0)
                         for c in range(nc)])
            vals.append([jnp.concatenate([xv[c]] + [xv[j] for j in range(c - 1, -1, -1)], axis=0) for c in range(nc)])
            qg.append([stack(xq[c], hq) for c in range(nc)]
                      + [jnp.concatenate([stack(xq[c] * span(0, c), hq) for c in range(nc)], axis=0)])
            kgs.append((jnp.concatenate(xv, axis=0),
                        jnp.concatenate([stack(xkg[c] * span(c + 1, nc), hq) for c in range(nc)], axis=0)))
            gtot.append(span(0, nc))
        att = [[jnp.where(masks[c], _mm_nt(qg[bi][c], keys[bi][c]), 0.0) for c in range(nc)] for bi in seqs]
        upd = [_mm_tn(*kgs[bi]) for bi in seqs]
        st = [st_ref[bi] for bi in seqs]
        oi = [_mm_nt(qg[bi][nc], st[bi]) for bi in seqs]
        for bi in seqs:
            for c in range(nc):
                oo = _mm(att[bi][c], vals[bi][c]) + oi[bi][c * nh * L:(c + 1) * nh * L]
                acc = oo[:L]
                for h in range(1, nh):
                    acc = acc + oo[h * L:(h + 1) * L]
                o_s[pl.ds(bi * tt + r0 + L * c, L), :] = acc
            st_ref[bi] = st[bi] * gtot[bi] + upd[bi]
        return carry

    for g in range(tt // (nc * L)):
        body(g, 0)

    for h in range(nh):
        cs = slice(h * LANES, (h + 1) * LANES)
        oh = o_s[:, cs]
        oh = oh * lax.rsqrt(jnp.mean(oh * oh, axis=-1, keepdims=True) + EPS) * gn_ref[...]
        out_ref[:, :, cs] = (oh * og_s[:, cs]).astype(BF16).reshape(nb, tt, LANES)


def _mlstm_kernel(u_ref, w_ref, cw_ref, cb_ref, ib_ref, fbias_ref, gn_ref, sel_ref, eli_ref, eb_ref,
                  out_ref, c_ref, m_ref, prev_ref, q_s, k_s, v_s, bf_s, lf_s, h_s, og_s, *, nb, tt):
    L = MLSTM_CHUNK
    nh = 4
    _first_step_zero(c_ref, m_ref, prev_ref)

    u = u_ref[...].reshape(nb * tt, D_MODEL)
    misc = jnp.dot(u, w_ref[:, 1536:1664], preferred_element_type=F32)
    fc = jnp.dot(u, w_ref[:, :1536], preferred_element_type=F32)
    qk = fc[:, :512]
    conv = _causal_conv(qk, [prev_ref[b] for b in range(nb)], cw_ref[...], cb_ref[...], tt)
    _save_carry(prev_ref, qk, nb, tt)
    qk = _silu(conv)
    q_s[...] = qk[:, :256].astype(BF16)
    k_s[...] = qk[:, 256:] * 0.125
    v_s[...] = fc[:, 512:1024].astype(BF16)
    og_s[...] = jax.nn.sigmoid(fc[:, 1024:1536])
    lane = lax.broadcasted_iota(jnp.int32, (1, LANES), 1)
    li = misc + ib_ref[...]
    b = _chunk_cumsum(sel_ref[...], _log_sigmoid(misc + fbias_ref[...]), nb, tt)
    lf_s[...] = _sel_rhs(li, eli_ref[...], 2)
    bf_s[...] = _sel_rhs(b, eb_ref[...], 3)

    hq = lax.broadcasted_iota(jnp.int32, (1, 256), 1) // 64
    ri = lax.broadcasted_iota(jnp.int32, (nh * L, nh * L), 0)
    ci = lax.broadcasted_iota(jnp.int32, (nh * L, nh * L), 1)
    dmask = jnp.logical_and((ri // L) == (ci // L), (ri % L) >= (ci % L))
    ones = jnp.ones((L, LANES), BF16)
    seqs = range(nb)

    def stackq(x):
        zero = jnp.zeros_like(x)
        return jnp.concatenate([jnp.where(hq == h, x, zero) for h in range(nh)], axis=0)

    def headcol(ref, rs):
        x = ref[rs, :]
        return jnp.concatenate([x[:, h * LANES:(h + 1) * LANES] for h in range(nh)], axis=0)

    def perhead_last(col):
        return jnp.concatenate(
            [jnp.broadcast_to(col[(h + 1) * L - 1:(h + 1) * L, :], (L, LANES)) for h in range(nh)], axis=0)

    def wide(col):
        return jnp.concatenate([col, col], axis=1)

    def body(c, carry):
        r0 = c * L
        rows = [pl.ds(bi * tt + r0, L) for bi in seqs]
        xq = [stackq(q_s[rs, :]) for rs in rows]
        kf = [stackq(k_s[rs, :]) for rs in rows]
        xv = []
        for rs in rows:
            vc = v_s[rs, :]
            xv.append(jnp.concatenate(
                [jnp.concatenate([vc[:, h * LANES:(h + 1) * LANES], ones], axis=1) for h in range(nh)], axis=0))
        qk = [_mm_nt(xq[bi], kf[bi]) for bi in seqs]
        qc = [_mm(xq[bi], c_ref[bi]) for bi in seqs]
        sc, m_ts, inters, w_lasts = [], [], [], []
        for bi in seqs:
            rs = rows[bi]
            bcol, licol = headcol(bf_s, rs), headcol(lf_s, rs)
            mcol = m_ref[bi]
            zt = (licol - bcol).T
            log_d = jnp.where(dmask, wide(bcol) + jnp.concatenate([zt, zt], axis=0), -jnp.inf)
            mx = jnp.broadcast_to(jnp.max(log_d, axis=-1, keepdims=True), (nh * L, LANES))
            m_t = jnp.maximum(bcol + mcol, mx)
            sc.append((qk[bi] * jnp.exp(log_d - wide(m_t))).astype(BF16))
            m_ts.append(m_t)
            inters.append(jnp.exp(bcol + mcol - m_t))
            w_lasts.append(jnp.exp(perhead_last(bcol) - bcol + licol - perhead_last(m_t)))
        num = [_mm(sc[bi], xv[bi]) for bi in seqs]
        upd = [_mm_tn(kf[bi] * wide(w_lasts[bi]), xv[bi]) for bi in seqs]
        for bi in seqs:
            tot = num[bi] + qc[bi] * wide(inters[bi])
            hh = tot[:, :LANES] / jnp.maximum(jnp.abs(tot[:, LANES:]), jnp.exp(-m_ts[bi]))
            for h in range(nh):
                h_s[rows[bi], h * LANES:(h + 1) * LANES] = hh[h * L:(h + 1) * L, :]
            c_ref[bi] = wide(perhead_last(inters[bi])) * c_ref[bi] + upd[bi]
            m_ref[bi] = perhead_last(m_ts[bi])
        return carry

    for c in range(tt // L):
        body(c, 0)

    for h in range(nh):
        cs = slice(h * LANES, (h + 1) * LANES)
        hh = h_s[:, cs]
        hh = hh * lax.rsqrt(jnp.mean(hh * hh, axis=-1, keepdims=True) + EPS) * gn_ref[:, cs]
        out_ref[:, :, cs] = (hh * og_s[:, cs]).astype(BF16).reshape(nb, tt, LANES)


def _ssd_kernel(u_ref, w_ref, cw_ref, cb_ref, dtb_ref, alog_ref, dsk_ref, gn_ref, sel_ref,
                ex_ref, out_ref, st_ref, prev_ref, *, nb, tt):
    L = SSD_CHUNK
    assert tt == L
    _first_step_zero(st_ref, prev_ref)

    u = u_ref[...].reshape(nb * tt, D_MODEL)
    dt_pre = jnp.dot(u, w_ref[:, 1536:1664], preferred_element_type=F32)
    fd = jnp.dot(u, w_ref[:, :1536], preferred_element_type=F32)
    z = _silu(fd[:, :512])
    xbc = fd[:, 512:1536]
    conv = _silu(_causal_conv(xbc, [prev_ref[b] for b in range(nb)], cw_ref[...], cb_ref[...], tt))
    _save_carry(prev_ref, xbc, nb, tt)
    x = conv[:, :512]
    bm = conv[:, 512:768].astype(BF16)
    cm = conv[:, 768:1024].astype(BF16)
    lane = lax.broadcasted_iota(jnp.int32, (1, LANES), 1)
    valid = lane < 8
    dt = jnp.where(valid, _softplus(dt_pre + dtb_ref[...]), 0.0)
    dta = dt * jnp.where(valid, -jnp.exp(alog_ref[...]), 0.0)
    ac = _chunk_cumsum(sel_ref[...], dta, nb, tt, 3)
    ace = _sel_rhs(ac, ex_ref[...], 3)
    xd = x * _sel_rhs(dt, ex_ref[...], 2)

    ri = lax.broadcasted_iota(jnp.int32, (L, L), 0)
    ci = lax.broadcasted_iota(jnp.int32, (L, L), 1)
    tril = ri >= ci
    m0 = lane < 64
    npair = MIX // LANES
    seqs = range(nb)
    seg = lambda a, bi: a[bi * tt:(bi + 1) * tt]
    groups = [(bi, g) for bi in seqs for g in range(2)]
    chains = [(bi, p) for bi in seqs for p in range(npair)]

    cb = {(bi, g): _mm_nt(seg(cm, bi)[:, g * LANES:(g + 1) * LANES], seg(bm, bi)[:, g * LANES:(g + 1) * LANES])
          for bi, g in groups}
    last = {bi: seg(ace, bi)[L - 1:L, :] for bi in seqs}
    ys = {}
    for bi, p in chains:
        cs = slice(p * LANES, (p + 1) * LANES)
        gs = slice((p // 2) * LANES, (p // 2 + 1) * LANES)
        ys[bi, p] = _mm(seg(cm, bi)[:, gs], st_ref[bi, p]) * jnp.exp(seg(ace, bi)[:, cs])
    for bi, p in chains:
        cs = slice(p * LANES, (p + 1) * LANES)
        gs = slice((p // 2) * LANES, (p // 2 + 1) * LANES)
        xdec = seg(xd, bi)[:, cs] * jnp.exp(last[bi][:, cs] - seg(ace, bi)[:, cs])
        st_ref[bi, p] = st_ref[bi, p] * jnp.exp(last[bi][:, cs]) + _mm_tn(seg(bm, bi)[:, gs], xdec)
    for bi in seqs:
        acc = seg(ac, bi)
        act = acc.T
        for p in range(npair):
            cs = slice(p * LANES, (p + 1) * LANES)
            xp = seg(xd, bi)[:, cs]
            y = ys[bi, p]
            for hh, xm in ((2 * p, jnp.where(m0, xp, 0.0)), (2 * p + 1, jnp.where(m0, 0.0, xp))):
                sg = jnp.exp(jnp.where(tril, acc[:, hh:hh + 1] - act[hh:hh + 1, :], -jnp.inf))
                y = y + _mm(cb[bi, p // 2] * sg, xm)
            ys[bi, p] = y
    y = jnp.concatenate([jnp.concatenate([ys[bi, p] for p in range(npair)], axis=1) for bi in seqs], axis=0)
    y = (y + x * dsk_ref[...]) * z
    out_ref[...] = _rms(y, gn_ref[...]).astype(BF16).reshape(nb, tt, MIX)


def _merge_kernel(h_ref, u_ref, wg_ref, b0_ref, b1_ref, b2_ref, b3_ref, p_ref, wo_ref, gnext_ref,
                  out_ref, unext_ref):
    u = u_ref[...]
    acc = None
    for i, br in enumerate((b0_ref, b1_ref, b2_ref, b3_ref)):
        gate = jax.nn.sigmoid(jnp.dot(u, wg_ref[:, i * D_MODEL:(i + 1) * D_MODEL], preferred_element_type=F32))
        term = gate * jnp.dot(br[...], p_ref[i], preferred_element_type=F32)
        acc = term if acc is None else acc + term
    h = h_ref[...] + _mm(acc, wo_ref[...])
    out_ref[...] = h
    unext_ref[...] = _rms(h, gnext_ref[...]).astype(BF16)


def _ffn_kernel(h_ref, u_ref, up_ref, cw_ref, cb_ref, dn_ref, gnext_ref, *rest, tm, tiles_per_seq, tf, last):
    out_refs, (a_s, carry_s) = rest[:-2], rest[-2:]
    u = u_ref[...]
    seq_start = pl.program_id(0) % tiles_per_seq == 0

    def half(off):
        cs = slice(off, off + tf)
        z = jnp.dot(u, up_ref[:, cs], preferred_element_type=F32)
        carry = jnp.where(seq_start, 0.0, carry_s[:, cs])
        carry_s[:, cs] = z[tm - CARRY_ROWS:, :]
        return _causal_conv(z, [carry], cw_ref[:, cs], cb_ref[:, cs], tm)

    for j in range(D_FF // tf):
        gate, val = half(j * tf), half(D_FF + j * tf)
        a_s[:, j * tf:(j + 1) * tf] = (_silu(gate) * val).astype(BF16)
    h = h_ref[...] + jnp.dot(a_s[...], dn_ref[...], preferred_element_type=F32)
    if last:
        out_refs[0][...] = _rms(h, gnext_ref[...])
    else:
        out_refs[0][...] = h
        out_refs[1][...] = _rms(h, gnext_ref[...]).astype(BF16)


def _norm_kernel(h_ref, g_ref, out_ref):
    out_ref[...] = _rms(h_ref[...], g_ref[...]).astype(out_ref.dtype)


def _layer_spec(arr, l):
    tail = arr.shape[1:]
    return pl.BlockSpec((None,) + tail, lambda *_, l=l, n=len(tail): (l,) + (0,) * n)


def _full_spec(arr):
    return pl.BlockSpec(arr.shape, lambda *_, n=arr.ndim: (0,) * n)


def _mixer_call(kern, name, h, l, layer_params, consts, scratch, tt):
    bsz, t, d = h.shape
    in_specs = [pl.BlockSpec((bsz, tt, d), lambda i: (0, i, 0))]
    in_specs += [_layer_spec(a, l) for a in layer_params]
    in_specs += [_full_spec(a) for a in consts]
    return pl.pallas_call(
        functools.partial(kern, nb=bsz, tt=tt),
        grid=(t // tt,),
        in_specs=in_specs,
        out_specs=pl.BlockSpec((bsz, tt, MIX), lambda i: (0, i, 0)),
        out_shape=jax.ShapeDtypeStruct((bsz, t, MIX), BF16),
        scratch_shapes=scratch,
        compiler_params=pltpu.CompilerParams(dimension_semantics=("arbitrary",),
                                             vmem_limit_bytes=VMEM_LIMIT),
        name=name,
    )(h, *layer_params, *consts)


def _vm(shape, dtype=F32):
    return pltpu.VMEM(shape, dtype)


def _block_diag_ones(n, blk, lower=False):
    i = np.arange(n)
    m = (i[:, None] // blk) == (i[None, :] // blk)
    if lower:
        m = m & (i[:, None] >= i[None, :])
    return m


def _chunk_selector(tt, chunk):
    return jnp.asarray(_block_diag_ones(tt, chunk, lower=True), BF16)


def _pad_lanes(a, width=LANES):
    return jnp.pad(a, [(0, 0)] * (a.ndim - 1) + [(0, width - a.shape[-1])])


def _row(a):
    return a[:, None, :]


def kernel(x, mix_norm_g, w_in, rwkv_mu, rwkv_w0, rwkv_w_up, rwkv_a0, rwkv_a_up, rwkv_g_up, rwkv_k_k, rwkv_k_a, rwkv_r_k, rwkv_gn_g, rwkv_gn_b, gla_f_up, gla_f_bias, gla_norm_g, mlstm_conv_w, mlstm_conv_b, mlstm_i_bias, mlstm_f_bias, mlstm_norm_g, ssd_conv_w, ssd_conv_b, ssd_dt_bias, ssd_a_log, ssd_d, ssd_norm_g, branch_proj, w_out, ffn_norm_g, ffn_up, ffn_conv_w, ffn_conv_b, ffn_down, final_norm_g):
    bsz, t, d = x.shape
    depth = w_in.shape[0]
    tt = MIXER_TILE
    rt = bsz * tt
    assert d == D_MODEL and t % tt == 0

    def cols(lo, hi):
        return w_in[:, :, lo:hi]

    zpad = lambda n: jnp.zeros((depth, d, n), F32)
    w_rwkv = cols(0, RWKV_IN).astype(BF16)
    o = OFF_GLA
    w_gla = jnp.concatenate([cols(o, o + 1024), cols(o + 1040, o + 1552), cols(o + 1024, o + 1040),
                             zpad(LANES - 16)], axis=-1).astype(BF16)
    o = OFF_MLSTM
    w_mlstm = jnp.concatenate([cols(o, o + 1024), cols(o + 1032, o + 1544), cols(o + 1024, o + 1032),
                               zpad(LANES - 8)], axis=-1).astype(BF16)
    o = OFF_SSD
    w_ssd = jnp.concatenate([cols(o, o + 1544), zpad(LANES - 8)], axis=-1).astype(BF16)
    w_gate = cols(OFF_GATE, OFF_GATE + N_BRANCH * d).astype(BF16)

    ng = _row(mix_norm_g)
    zl = jnp.zeros_like(rwkv_w_up)
    rwkv_lora = jnp.concatenate([jnp.concatenate([rwkv_w_up, zl], axis=-1),
                                 jnp.concatenate([zl, rwkv_a_up], axis=-1)], axis=1).astype(BF16)
    rwkv_params = [w_rwkv, _row(rwkv_mu), _row(rwkv_w0), _row(rwkv_a0), rwkv_lora,
                   rwkv_g_up.astype(BF16), _row(rwkv_k_k), _row(rwkv_k_a),
                   _row(rwkv_r_k.reshape(depth, MIX)), _row(rwkv_gn_g), _row(rwkv_gn_b)]
    rwkv_consts = [jnp.asarray(_block_diag_ones(2 * LANES, 64), BF16), _chunk_selector(tt, RWKV_CHUNK)]
    rwkv_scratch = ([_vm((bsz, MIX // LANES, LANES, LANES)), _vm((bsz, CARRY_ROWS, RWKV_IN))]
                    + [_vm((rt, MIX), BF16)] * 7 + [_vm((rt, MIX))] * 4)

    gla_fup = jnp.pad(gla_f_up, ((0, 0), (0, LANES - gla_f_up.shape[1]), (0, 0))).astype(BF16)
    gla_params = [w_gla, gla_fup, _row(gla_f_bias), _row(gla_norm_g)]
    gla_consts = [_chunk_selector(tt, GLA_CHUNK)]
    gla_scratch = [_vm((bsz, MIX, 256)), _vm((rt, 256)), _vm((rt, 256), BF16), _vm((rt, 256)),
                   _vm((rt, MIX), BF16), _vm((rt, 256)), _vm((rt, MIX)), _vm((rt, MIX))]

    zh = jnp.zeros_like(mlstm_i_bias)
    mlstm_params = [w_mlstm, mlstm_conv_w, _row(mlstm_conv_b),
                    _row(_pad_lanes(jnp.concatenate([mlstm_i_bias, zh], axis=-1))),
                    _row(_pad_lanes(jnp.concatenate([zh, mlstm_f_bias], axis=-1))),
                    _row(mlstm_norm_g)]
    tile_of_lane = np.arange(MIX) // LANES
    lane_to_tile = lambda off: jnp.asarray(np.arange(LANES)[:, None] == off + tile_of_lane[None, :], BF16)
    mlstm_consts = [_chunk_selector(tt, MLSTM_CHUNK), lane_to_tile(0), lane_to_tile(4)]
    mlstm_scratch = [_vm((bsz, 256, 256)), _vm((bsz, 256, LANES)), _vm((bsz, CARRY_ROWS, 512)),
                     _vm((rt, 256), BF16), _vm((rt, 256)), _vm((rt, MIX), BF16), _vm((rt, MIX)),
                     _vm((rt, MIX)), _vm((rt, MIX)), _vm((rt, MIX))]

    ssd_params = [w_ssd, ssd_conv_w, _row(ssd_conv_b), _row(_pad_lanes(ssd_dt_bias)),
                  _row(_pad_lanes(ssd_a_log)), _row(jnp.repeat(ssd_d, 64, axis=-1)), _row(ssd_norm_g)]
    hid = np.arange(MIX) // 64
    expand = jnp.asarray(np.arange(LANES)[:, None] == hid[None, :], BF16)
    ssd_consts = [_chunk_selector(tt, SSD_CHUNK), expand]
    ssd_scratch = [_vm((bsz, MIX // LANES, LANES, LANES)), _vm((bsz, CARRY_ROWS, 1024))]

    bproj = branch_proj.astype(BF16)
    wout = w_out.astype(BF16)
    fng = _row(ffn_norm_g)
    fup = ffn_up.astype(BF16)
    fdn = ffn_down.astype(BF16)
    fcb = _row(ffn_conv_b)

    m = bsz * t
    tm = 512
    tf = 2 * LANES
    assert t % tm == 0 and D_FF % tf == 0
    cparams = pltpu.CompilerParams(dimension_semantics=("arbitrary",), vmem_limit_bytes=VMEM_LIMIT)
    row_spec = pl.BlockSpec((tm, d), lambda i: (i, 0))
    br_spec = pl.BlockSpec((tm, MIX), lambda i: (i, 0))

    def resident(arr, l):
        tail = arr.shape[1:]
        return pl.BlockSpec((None,) + tail, lambda i, l=l, n=len(tail): (l,) + (0,) * n,
                            pipeline_mode=pl.Buffered(1))

    def merge(h2, u2, l, branches):
        return pl.pallas_call(
            _merge_kernel,
            grid=(m // tm,),
            in_specs=[row_spec, row_spec, resident(w_gate, l), br_spec, br_spec, br_spec, br_spec,
                      resident(bproj, l), resident(wout, l), resident(fng, l)],
            out_specs=[row_spec, row_spec],
            out_shape=[jax.ShapeDtypeStruct((m, d), F32), jax.ShapeDtypeStruct((m, d), BF16)],
            compiler_params=cparams,
            name="merge",
        )(h2, u2, w_gate, *branches, bproj, wout, fng)

    def ffn(h2, u2, l):
        last = l == depth - 1
        if last:
            gnext, gspec = final_norm_g[None, :], pl.BlockSpec((1, d), lambda i: (0, 0))
            out_specs, out_shape = row_spec, jax.ShapeDtypeStruct((m, d), x.dtype)
        else:
            gnext, gspec = ng, resident(ng, l + 1)
            out_specs = [row_spec, row_spec]
            out_shape = [jax.ShapeDtypeStruct((m, d), F32), jax.ShapeDtypeStruct((m, d), BF16)]
        return pl.pallas_call(
            functools.partial(_ffn_kernel, tm=tm, tiles_per_seq=t // tm, tf=tf, last=last),
            grid=(m // tm,),
            in_specs=[row_spec, row_spec, resident(fup, l), resident(ffn_conv_w, l), resident(fcb, l),
                      resident(fdn, l), gspec],
            out_specs=out_specs,
            out_shape=out_shape,
            scratch_shapes=[_vm((tm, D_FF), BF16), _vm((CARRY_ROWS, 2 * D_FF))],
            compiler_params=cparams,
            name="ffn",
        )(h2, u2, fup, ffn_conv_w, fcb, fdn, gnext)

    h = x.reshape(m, d)
    u = pl.pallas_call(
        _norm_kernel,
        grid=(m // tm,),
        in_specs=[row_spec, resident(ng, 0)],
        out_specs=row_spec,
        out_shape=jax.ShapeDtypeStruct((m, d), BF16),
        name="first_norm",
    )(h, ng)
    for l in range(depth):
        u3 = u.reshape(bsz, t, d)
        branches = [
            _mixer_call(_rwkv_kernel, "rwkv", u3, l, rwkv_params, rwkv_consts, rwkv_scratch, tt),
            _mixer_call(_gla_kernel, "gla", u3, l, gla_params, gla_consts, gla_scratch, tt),
            _mixer_call(_mlstm_kernel, "mlstm", u3, l, mlstm_params, mlstm_consts, mlstm_scratch, tt),
            _mixer_call(_ssd_kernel, "ssd", u3, l, ssd_params, ssd_consts, ssd_scratch, tt),
        ]
        h, u = merge(h, u, l, [b.reshape(m, MIX) for b in branches])
        if l == depth - 1:
            h = ffn(h, u, l)
        else:
            h, u = ffn(h, u, l)
    return h.reshape(bsz, t, d)
```
